```python
import math
import jax, jax.numpy as jnp
from jax import lax
import numpy as np

D_MODEL = 1024
BATCH = 1
SEQ = 16384
DEPTH = 4

HEAD_DIM = 64
BLOCK = 128
DIL_GROUPS = ((128, 1), (512, 4), (2048, 16))
N_GROUPS = 3
A_HEADS_PER_GROUP = 4
A_HEADS = N_GROUPS * A_HEADS_PER_GROUP
A_WIDTH = A_HEADS * HEAD_DIM
A_OUT = A_HEADS_PER_GROUP * HEAD_DIM
SB_HEADS = 4
SB_HEAD_DIM = 128
SB_WIDTH = SB_HEADS * SB_HEAD_DIM
RET_HEADS = 4
RET_DK = 64
RET_DV = 128
RET_QK_WIDTH = RET_HEADS * RET_DK
RET_V_WIDTH = RET_HEADS * RET_DV
RET_CHUNK = 128
ROPE_BASE = 10000.0
REL_BUCKETS = 32
REL_MAX_DIST = 2048
N_BRANCH = 3
D_FF = 2816
CONV_WIDTH = 3
EPS = 1e-6

IN_SIZES = (A_WIDTH,) * 3 + (SB_WIDTH,) * 3 + (RET_QK_WIDTH,) * 2 + (RET_V_WIDTH,) * 2 + (N_BRANCH * D_MODEL,)
IN_WIDTH = 3 * A_WIDTH + 3 * SB_WIDTH + 2 * RET_QK_WIDTH + 2 * RET_V_WIDTH + N_BRANCH * D_MODEL

kernel_name = "hybrid_dilated_stickbreak_retention_convffn"


def rms_norm(x, w):
    xf = x.astype(jnp.float32)
    y = xf * lax.rsqrt(jnp.mean(xf * xf, axis=-1, keepdims=True) + EPS)
    return (y * w.astype(jnp.float32)).astype(x.dtype)


def t5_causal_bucket(dist):
    max_exact = REL_BUCKETS // 2
    n = np.maximum(dist, 0)
    large = max_exact + (np.log(np.maximum(n, 1) / max_exact) / np.log(REL_MAX_DIST / max_exact)
                         * (REL_BUCKETS - max_exact)).astype(np.int64)
    large = np.minimum(large, REL_BUCKETS - 1)
    return np.where(n < max_exact, n, large).astype(np.int32)


def dilated_window_attention(q, k, v, bias_hd, window, dilation):
    b, s, h, hd = q.shape
    n_steps = window // dilation
    span = dilation * BLOCK
    sp = -(-s // span) * span
    L = sp // dilation
    nb = L // BLOCK

    def to_blocks(t):
        t = jnp.pad(t, ((0, 0), (0, sp - s), (0, 0), (0, 0))).reshape(b, L, dilation, h, hd)
        return t.transpose(0, 2, 1, 3, 4).reshape(b, dilation, nb, BLOCK, h, hd)

    def with_prev(t):
        prev = jnp.pad(t, ((0, 0), (0, 0), (1, 0), (0, 0), (0, 0), (0, 0)))[:, :, :-1]
        return jnp.concatenate([prev, t], axis=3)

    qb = to_blocks(q)
    kw = with_prev(to_blocks(k))
    vw = with_prev(to_blocks(v))

    a_idx = np.arange(BLOCK)[:, None]
    c_idx = np.arange(2 * BLOCK)[None, :]
    steps = a_idx + BLOCK - c_idx
    in_band = (steps >= 0) & (steps <= n_steps)
    valid = in_band[None] & ((np.arange(nb)[:, None, None] > 0) | (c_idx[None] >= BLOCK))
    bucket = t5_causal_bucket(steps * dilation)
    bias = jnp.take(bias_hd.astype(jnp.float32), jnp.asarray(bucket), axis=0).transpose(2, 0, 1)

    logits = jnp.einsum('bdnqhc,bdnkhc->bdnhqk', qb, kw).astype(jnp.float32) * (hd ** -0.5) + bias
    logits = jnp.where(valid[None, None, :, None], logits, -jnp.inf)
    lse = jax.nn.logsumexp(logits, axis=-1)
    p = jnp.exp(logits - lse[..., None])
    out = jnp.einsum('bdnhqk,bdnkhc->bdnqhc', p.astype(v.dtype), vw)

    out = out.reshape(b, dilation, L, h, hd).transpose(0, 2, 1, 3, 4).reshape(b, sp, h, hd)[:, :s]
    lse = lse.transpose(0, 1, 2, 4, 3).reshape(b, dilation, L, h).transpose(0, 2, 1, 3).reshape(b, sp, h)[:, :s]
    return out, lse


def stick_breaking_attention(q, k, v):
    b, s, h, hd = q.shape
    nb = s // BLOCK
    scale = hd ** -0.5
    kb = k.reshape(b, nb, BLOCK, h, hd)
    vb = v.reshape(b, nb, BLOCK, h, hd)
    tri_in = jnp.asarray(np.arange(BLOCK)[:, None] > np.arange(BLOCK)[None, :], jnp.float32)
    diag_mask = np.arange(BLOCK)[None, :] < np.arange(BLOCK)[:, None]
    outs = []
    for i in range(nb):
        n = i + 1
        qi = q[:, i * BLOCK:(i + 1) * BLOCK]
        z = jnp.einsum('bqhc,bnkhc->bhqnk', qi, kb[:, :n]).astype(jnp.float32) * scale
        mask_np = np.ones((BLOCK, n, BLOCK), dtype=bool)
        mask_np[:, -1, :] = diag_mask
        mask = jnp.asarray(mask_np)
        log_keep = jnp.where(mask, jax.nn.log_sigmoid(-z), 0.0)
        later_in = jnp.einsum('bhqnk,kj->bhqnj', log_keep, tri_in)
        tri_blk = jnp.asarray(np.arange(n)[:, None] > np.arange(n)[None, :], jnp.float32)
        later_blk = jnp.einsum('bhqm,mn->bhqn', jnp.sum(log_keep, axis=-1), tri_blk)
        log_w = jnp.where(mask, log_keep + z + later_in + later_blk[..., None], -jnp.inf)
        w = jnp.exp(log_w)
        outs.append(jnp.einsum('bhqnk,bnkhc->bqhc', w.astype(v.dtype), vb[:, :n]))
    return jnp.concatenate(outs, axis=1)


def rotary(x, pos):
    half = x.shape[-1] // 2
    inv = ROPE_BASE ** (-jnp.arange(half, dtype=jnp.float32) / half)
    ang = pos.astype(jnp.float32)[:, None] * inv[None, :]
    cos = jnp.cos(ang)[None, :, None, :]
    sin = jnp.sin(ang)[None, :, None, :]
    x1 = x[..., :half].astype(jnp.float32)
    x2 = x[..., half:].astype(jnp.float32)
    return jnp.concatenate([x1 * cos - x2 * sin, x1 * sin + x2 * cos], axis=-1)


def retention(q, k, v):
    b, s, h, dk = q.shape
    dv = v.shape[-1]
    C = RET_CHUNK
    nc = s // C
    log_gamma = jnp.log1p(-jnp.exp2(-5.0 - jnp.arange(h, dtype=jnp.float32)))
    pos = jnp.arange(s, dtype=jnp.int32)
    qc = rotary(q, pos).reshape(b, nc, C, h, dk)
    kc = (rotary(k, pos) * (dk ** -0.5)).reshape(b, nc, C, h, dk)
    vc = v.astype(jnp.float32).reshape(b, nc, C, h, dv)

    n = jnp.arange(C, dtype=jnp.float32)
    diff = n[:, None] - n[None, :]
    decay_intra = jnp.where(diff >= 0, jnp.exp(diff[None] * log_gamma[:, None, None]), 0.0)
    intra = jnp.einsum('bnqhd,bnkhd->bnhqk', qc, kc) * decay_intra
    o_intra = jnp.einsum('bnhqk,bnkhe->bnqhe', intra, vc)

    zeta = jnp.exp((C - 1 - n)[:, None] * log_gamma[None, :])
    kv = jnp.einsum('bnkhd,bnkhe->bnhde', kc * zeta[None, None, :, :, None], vc)
    chunk_decay = jnp.exp(C * log_gamma)[:, None, None]

    def step(r, kv_i):
        return r * chunk_decay + kv_i, r

    _, r_prev = lax.scan(step, jnp.zeros((b, h, dk, dv), jnp.float32), kv.transpose(1, 0, 2, 3, 4))
    r_prev = r_prev.transpose(1, 0, 2, 3, 4)
    xi = jnp.exp((n + 1)[:, None] * log_gamma[None, :])
    o_cross = jnp.einsum('bnqhd,bnhde->bnqhe', qc * xi[None, None, :, :, None], r_prev)
    return (o_intra + o_cross).reshape(b, s, h, dv)


def conv_ffn(h, w_up, w_gate, conv_w, conv_b, w_down):
    u = h @ w_up
    u = lax.conv_general_dilated(u, conv_w[:, None, :].astype(u.dtype), window_strides=(1,),
                                 padding=((CONV_WIDTH - 1, 0),),
                                 dimension_numbers=('NWC', 'WIO', 'NWC'),
                                 feature_group_count=D_FF) + conv_b
    return (jax.nn.gelu(u) * (h @ w_gate)) @ w_down


def setup_inputs(seed: int = 0) -> dict:
    key = jax.random.key(seed)
    ks = jax.random.split(key, 20)
    f32 = jnp.float32
    nrm = lambda k, shape, fan: jax.random.normal(k, shape, f32) * (fan ** -0.5)
    return {
        "x": jax.random.normal(ks[0], (BATCH, SEQ, D_MODEL), f32),
        "rel_bias": jax.random.normal(ks[1], (REL_BUCKETS, A_HEADS), f32) * 0.5,
        "norm_mix_w": 1.0 + 0.01 * jax.random.normal(ks[2], (DEPTH, D_MODEL), f32),
        "w_in": nrm(ks[3], (DEPTH, D_MODEL, IN_WIDTH), D_MODEL),
        "b_gate": 0.1 * jax.random.normal(ks[4], (DEPTH, N_BRANCH, D_MODEL), f32),
        "ret_norm_w": 1.0 + 0.01 * jax.random.normal(ks[5], (DEPTH, RET_V_WIDTH), f32),
        "w_proj_a": nrm(ks[6], (DEPTH, A_OUT, D_MODEL), A_OUT),
        "w_proj_b": nrm(ks[7], (DEPTH, SB_WIDTH, D_MODEL), SB_WIDTH),
        "w_proj_c": nrm(ks[8], (DEPTH, RET_V_WIDTH, D_MODEL), RET_V_WIDTH),
        "w_out": nrm(ks[9], (DEPTH, D_MODEL, D_MODEL), D_MODEL),
        "norm_ffn_w": 1.0 + 0.01 * jax.random.normal(ks[10], (DEPTH, D_MODEL), f32),
        "w_up": nrm(ks[11], (DEPTH, D_MODEL, D_FF), D_MODEL),
        "w_gate": nrm(ks[12], (DEPTH, D_MODEL, D_FF), D_MODEL),
        "conv_w": nrm(ks[13], (DEPTH, CONV_WIDTH, D_FF), CONV_WIDTH),
        "conv_b": 0.01 * jax.random.normal(ks[14], (DEPTH, D_FF), f32),
        "w_down": nrm(ks[15], (DEPTH, D_FF, D_MODEL), D_FF),
        "final_norm_w": 1.0 + 0.01 * jax.random.normal(ks[16], (D_MODEL,), f32),
    }


def reference(x, rel_bias, norm_mix_w, w_in, b_gate, ret_norm_w, w_proj_a, w_proj_b, w_proj_c,
              w_out, norm_ffn_w, w_up, w_gate, conv_w, conv_b, w_down, final_norm_w):
    b, s, _ = x.shape
    offsets = np.cumsum(IN_SIZES)[:-1].tolist()
    for l in range(DEPTH):
        h = rms_norm(x, norm_mix_w[l])
        proj = h @ w_in[l]
        (a_q, a_k, a_v, sb_q, sb_k, sb_v, r_q, r_k, r_v, r_g, gate_in) = jnp.split(proj, offsets, axis=-1)

        a_q = a_q.reshape(b, s, N_GROUPS, A_HEADS_PER_GROUP, HEAD_DIM)
        a_k = a_k.reshape(b, s, N_GROUPS, A_HEADS_PER_GROUP, HEAD_DIM)
        a_v = a_v.reshape(b, s, N_GROUPS, A_HEADS_PER_GROUP, HEAD_DIM)
        outs, lses = [], []
        for g, (window, dilation) in enumerate(DIL_GROUPS):
            bias_g = rel_bias[:, g * A_HEADS_PER_GROUP:(g + 1) * A_HEADS_PER_GROUP]
            o_g, lse_g = dilated_window_attention(a_q[:, :, g], a_k[:, :, g], a_v[:, :, g], bias_g, window, dilation)
            outs.append(o_g)
            lses.append(lse_g)
        wts = jax.nn.softmax(jnp.stack(lses), axis=0)
        y_a = jnp.einsum('gbsh,gbshc->bshc', wts.astype(x.dtype), jnp.stack(outs)).reshape(b, s, A_OUT)

        y_b = stick_breaking_attention(sb_q.reshape(b, s, SB_HEADS, SB_HEAD_DIM),
                                       sb_k.reshape(b, s, SB_HEADS, SB_HEAD_DIM),
                                       sb_v.reshape(b, s, SB_HEADS, SB_HEAD_DIM)).reshape(b, s, SB_WIDTH)

        o_r = retention(r_q.reshape(b, s, RET_HEADS, RET_DK), r_k.reshape(b, s, RET_HEADS, RET_DK),
                        r_v.reshape(b, s, RET_HEADS, RET_DV))
        mu = jnp.mean(o_r, axis=-1, keepdims=True)
        var = jnp.mean(jnp.square(o_r - mu), axis=-1, keepdims=True)
        o_r = ((o_r - mu) * lax.rsqrt(var + EPS)).reshape(b, s, RET_V_WIDTH) * ret_norm_w[l].astype(jnp.float32)
        y_c = jax.nn.silu(r_g) * o_r.astype(x.dtype)

        gates = jax.nn.sigmoid(gate_in.reshape(b, s, N_BRANCH, D_MODEL) + b_gate[l])
        merged = (gates[:, :, 0] * (y_a @ w_proj_a[l]) + gates[:, :, 1] * (y_b @ w_proj_b[l])
                  + gates[:, :, 2] * (y_c @ w_proj_c[l]))
        x = x + merged @ w_out[l]

        h = rms_norm(x, norm_ffn_w[l])
        x = x + conv_ffn(h, w_up[l], w_gate[l], conv_w[l], conv_b[l], w_down[l])
    return rms_norm(x, final_norm_w)
```

```python
import functools

import numpy as np
import jax
import jax.numpy as jnp
from jax import lax
from jax.experimental import pallas as pl
from jax.experimental.pallas import tpu as pltpu

F32 = jnp.float32
BF16 = jnp.bfloat16

D_MODEL = 1024
DEPTH = 4
HEAD_DIM = 64
BLOCK = 128
DIL_GROUPS = ((128, 1), (512, 4), (2048, 16))
N_GROUPS = 3
A_HEADS_PER_GROUP = 4
A_WIDTH = 768
A_OUT = 256
SB_HEADS = 4
SB_HEAD_DIM = 128
SB_WIDTH = 512
RET_HEADS = 4
RET_DK = 64
RET_DV = 128
RET_QK_WIDTH = 256
RET_V_WIDTH = 512
RET_CHUNK = 128
ROPE_BASE = 10000.0
REL_BUCKETS = 32
REL_MAX_DIST = 2048
N_BRANCH = 3
D_FF = 2816
EPS = 1e-6

GATE_W = N_BRANCH * D_MODEL
MAIN_W = 3 * A_WIDTH + 3 * SB_WIDTH + 2 * RET_QK_WIDTH + 2 * RET_V_WIDTH
IN_WIDTH = MAIN_W + GATE_W
OFF_AQ, OFF_AK, OFF_AV = 0, 768, 1536
OFF_SQ, OFF_SK, OFF_SV = 2304, 2816, 3328
OFF_RQ, OFF_RK, OFF_RV, OFF_RG = 3840, 4096, 4352, 4864

NEG = -1e30
VMEM_LIMIT = 56 * 1024 * 1024

NT_DIMS = (((1,), (1,)), ((), ()))
TN_DIMS = (((0,), (0,)), ((), ()))


def _params(sem, vmem=VMEM_LIMIT):
    return pltpu.CompilerParams(dimension_semantics=sem, vmem_limit_bytes=vmem)


def _const_spec(shape, index):
    nd = len(index)
    return pl.BlockSpec(shape, lambda *_: index, pipeline_mode=pl.Buffered(1))


IN_TM = 512
IN_CH = 768


def _inproj_body(x_ref, nw_ref, sc_ref, w_ref, pm_ref, gi_ref):
    x = x_ref[...]
    h = (x * lax.rsqrt(jnp.mean(x * x, axis=-1, keepdims=True) + EPS) * nw_ref[...]).astype(BF16)
    for c in range(MAIN_W // IN_CH):
        sl = slice(c * IN_CH, (c + 1) * IN_CH)
        y = jnp.dot(h, w_ref[:, sl], preferred_element_type=F32)
        pm_ref[:, sl] = (y * sc_ref[:, sl]).astype(BF16)
    for c in range(GATE_W // IN_CH):
        y = jnp.dot(h, w_ref[:, MAIN_W + c * IN_CH:MAIN_W + (c + 1) * IN_CH], preferred_element_type=F32)
        gi_ref[:, c * IN_CH:(c + 1) * IN_CH] = y.astype(BF16)


def _inproj(x, norm_w3, scale_row, w_in_bf, l):
    s = x.shape[0]
    return pl.pallas_call(
        _inproj_body,
        grid=(s // IN_TM,),
        in_specs=[
            pl.BlockSpec((IN_TM, D_MODEL), lambda i: (i, 0)),
            _const_spec((None, 1, D_MODEL), (l, 0, 0)),
            _const_spec((1, MAIN_W), (0, 0)),
            _const_spec((None, D_MODEL, IN_WIDTH), (l, 0, 0)),
        ],
        out_specs=[
            pl.BlockSpec((IN_TM, MAIN_W), lambda i: (i, 0)),
            pl.BlockSpec((IN_TM, GATE_W), lambda i: (i, 0)),
        ],
        out_shape=[jax.ShapeDtypeStruct((s, MAIN_W), BF16), jax.ShapeDtypeStruct((s, GATE_W), BF16)],
        compiler_params=_params(("parallel",)),
        name="inproj",
    )(x, norm_w3, scale_row, w_in_bf)


def _dil_body(q_ref, kc_ref, kp_ref, vc_ref, vp_ref, bias_ref, o_ref, lse_ref):
    n = pl.program_id(1)
    lane = lax.broadcasted_iota(jnp.int32, (BLOCK, 128), 1)
    lo = lane < HEAD_DIM
    prev_ok = n > 0
    for p in range(2):
        sl = slice(p * 128, (p + 1) * 128)
        q = q_ref[:, sl]
        kc, kp, vc, vp = kc_ref[:, sl], kp_ref[:, sl], vc_ref[:, sl], vp_ref[:, sl]
        outs, lses = [], []
        for hh in range(2):
            h = 2 * p + hh
            sel = lo if hh == 0 else jnp.logical_not(lo)
            qm = jnp.where(sel, q, jnp.zeros_like(q))
            s_cur = lax.dot_general(qm, kc, NT_DIMS, preferred_element_type=F32) + bias_ref[h, :, BLOCK:]
            s_prev = lax.dot_general(qm, kp, NT_DIMS, preferred_element_type=F32) + bias_ref[h, :, :BLOCK]
            s_prev = jnp.where(prev_ok, s_prev, NEG)
            m = jnp.maximum(jnp.max(s_cur, axis=-1, keepdims=True), jnp.max(s_prev, axis=-1, keepdims=True))
            p_cur = jnp.exp(s_cur - m)
            p_prev = jnp.exp(s_prev - m)
            den = jnp.sum(p_cur, axis=-1, keepdims=True) + jnp.sum(p_prev, axis=-1, keepdims=True)
            num = (jnp.dot(p_cur.astype(BF16), vc, preferred_element_type=F32)
                   + jnp.dot(p_prev.astype(BF16), vp, preferred_element_type=F32))
            outs.append(num / den)
            lses.append(jnp.broadcast_to(m + jnp.log(den), (BLOCK, 128)))
        o_ref[:, sl] = jnp.where(lo, outs[0], outs[1])
        lse_ref[:, sl] = jnp.where(lo, lses[0], lses[1])


def _dilated_group(pm, bias_g, g, dilation):
    s = pm.shape[0]
    rows = s // dilation
    nb = rows // BLOCK
    pmv = pm.reshape(rows, dilation * MAIN_W)
    cb = MAIN_W // 256
    qb, kb, vb = OFF_AQ // 256 + g, OFF_AK // 256 + g, OFF_AV // 256 + g

    def cur(off):
        return pl.BlockSpec((BLOCK, 256), lambda r, n: (n, r * cb + off))

    def prev(off):
        return pl.BlockSpec((BLOCK, 256), lambda r, n: (jnp.maximum(n - 1, 0), r * cb + off))

    o, lse = pl.pallas_call(
        _dil_body,
        grid=(dilation, nb),
        in_specs=[cur(qb), cur(kb), prev(kb), cur(vb), prev(vb),
                  _const_spec((A_HEADS_PER_GROUP, BLOCK, 2 * BLOCK), (0, 0, 0))],
        out_specs=[pl.BlockSpec((BLOCK, 256), lambda r, n: (n, r)),
                   pl.BlockSpec((BLOCK, 256), lambda r, n: (n, r))],
        out_shape=[jax.ShapeDtypeStruct((rows, dilation * A_OUT), F32),
                   jax.ShapeDtypeStruct((rows, dilation * A_OUT), F32)],
        compiler_params=_params(("parallel", "parallel")),
        name=f"dilated_attn_d{dilation}",
    )(pmv, pmv, pmv, pmv, pmv, bias_g)
    return o.reshape(s, A_OUT), lse.reshape(s, A_OUT)


SB_TQ = 256


def _sb_body(q_ref, k_ref, v_ref, tri_ref, o_ref, acc_ref, c_ref):
    i = pl.program_id(1)
    q = q_ref[...]
    acc_ref[...] = jnp.zeros_like(acc_ref)
    c_ref[...] = jnp.zeros_like(c_ref)
    row = lax.broadcasted_iota(jnp.int32, (SB_TQ, BLOCK), 0)
    col = lax.broadcasted_iota(jnp.int32, (SB_TQ, BLOCK), 1)

    def block(kj, mask):
        kb = k_ref[pl.ds(kj, BLOCK), :]
        vb = v_ref[pl.ds(kj, BLOCK), :]
        z = lax.dot_general(q, kb, NT_DIMS, preferred_element_type=F32)
        sp = jnp.maximum(z, 0.0) + jnp.log(1.0 + jnp.exp(-jnp.abs(z)))
        if mask is not None:
            sp = jnp.where(mask, sp, 0.0)
        t = jnp.dot(sp.astype(BF16), tri_ref[...], preferred_element_type=F32)
        lw = z - sp + t[:, :BLOCK] + c_ref[...]
        if mask is not None:
            lw = jnp.where(mask, lw, NEG)
        w = jnp.exp(lw)
        acc_ref[...] += jnp.dot(w.astype(BF16), vb, preferred_element_type=F32)
        c_ref[...] += t[:, BLOCK:]

    nd = SB_TQ // BLOCK
    for d in reversed(range(nd)):
        block(pl.multiple_of(i * SB_TQ + d * BLOCK, BLOCK), (col + d * BLOCK) < row)

    nfull = i * nd

    def body(t, carry):
        block(pl.multiple_of((nfull - 1 - t) * BLOCK, BLOCK), None)
        return carry

    lax.fori_loop(0, nfull, body, 0)
    o_ref[...] = acc_ref[...].astype(o_ref.dtype)


def _stick_breaking(pm, tri):
    s = pm.shape[0]
    qb, kb, vb = OFF_SQ // 128, OFF_SK // 128, OFF_SV // 128
    return pl.pallas_call(
        _sb_body,
        grid=(SB_HEADS, s // SB_TQ),
        in_specs=[
            pl.BlockSpec((SB_TQ, 128), lambda h, i: (i, qb + h)),
            pl.BlockSpec((s, 128), lambda h, i: (0, kb + h)),
            pl.BlockSpec((s, 128), lambda h, i: (0, vb + h)),
            _const_spec((BLOCK, 2 * BLOCK), (0, 0)),
        ],
        out_specs=pl.BlockSpec((SB_TQ, 128), lambda h, i: (i, h)),
        out_shape=jax.ShapeDtypeStruct((s, SB_WIDTH), BF16),
        scratch_shapes=[pltpu.VMEM((SB_TQ, BLOCK), F32), pltpu.VMEM((SB_TQ, BLOCK), F32)],
        compiler_params=_params(("parallel", "parallel")),
        name="stick_breaking",
    )(pm, pm, pm, tri)


def _ret_body(q_ref, k_ref, v01_ref, v23_ref, g01_ref, g23_ref, cos_ref, sin_ref, perm_ref, dec_ref,
              zeta_ref, xi_ref, cd_ref, nw_ref, o_ref, r_ref):
    n = pl.program_id(0)
    v_refs = (v01_ref, v23_ref)
    g_refs = (g01_ref, g23_ref)

    @pl.when(n == 0)
    def _():
        r_ref[...] = jnp.zeros_like(r_ref)

    cos = cos_ref[...]
    sin = sin_ref[...]
    perm = perm_ref[...]

    def rot(x):
        swapped = jnp.dot(x, perm, preferred_element_type=F32)
        return x.astype(F32) * cos + swapped * sin

    rq = rot(q_ref[...])
    rk = rot(k_ref[...])
    qb = rq.astype(BF16)
    qx = (rq * xi_ref[...]).astype(BF16)
    kb = rk.astype(BF16)
    kz = (rk * zeta_ref[...]).astype(BF16)
    lane = lax.broadcasted_iota(jnp.int32, (RET_CHUNK, 128), 1)
    lo = lane < RET_DK
    zero = jnp.zeros((RET_CHUNK, 128), BF16)
    for h in range(RET_HEADS):
        sl = slice((h // 2) * 128, (h // 2 + 1) * 128)
        sel = lo if h % 2 == 0 else jnp.logical_not(lo)
        hs = slice((h % 2) * RET_DV, (h % 2 + 1) * RET_DV)
        v = v_refs[h // 2][:, hs]
        qm = jnp.where(sel, qb[:, sl], zero)
        qxm = jnp.where(sel, qx[:, sl], zero)
        intra = lax.dot_general(qm, kb[:, sl], NT_DIMS, preferred_element_type=F32) * dec_ref[h]
        o = jnp.dot(intra.astype(BF16), v, preferred_element_type=F32)
        o = o + jnp.dot(qxm, r_ref[h].astype(BF16), preferred_element_type=F32)
        kv = lax.dot_general(kz[:, sl], v, TN_DIMS, preferred_element_type=F32)
        r_ref[h] = r_ref[h] * cd_ref[h] + kv
        mu = jnp.mean(o, axis=-1, keepdims=True)
        d = o - mu
        var = jnp.mean(d * d, axis=-1, keepdims=True)
        on = (d * lax.rsqrt(var + EPS)) * nw_ref[:, h * RET_DV:(h + 1) * RET_DV]
        gate = g_refs[h // 2][:, hs].astype(F32)
        o_ref[:, h * RET_DV:(h + 1) * RET_DV] = (gate * jax.nn.sigmoid(gate) * on).astype(o_ref.dtype)


def _retention(pm, tabs, ret_nw3, l):
    s = pm.shape[0]
    cos_t, sin_t, perm, dec, zeta, xi, cd = tabs
    c = RET_CHUNK
    return pl.pallas_call(
        _ret_body,
        grid=(s // c,),
        in_specs=[
            pl.BlockSpec((c, 256), lambda n: (n, OFF_RQ // 256)),
            pl.BlockSpec((c, 256), lambda n: (n, OFF_RK // 256)),
            pl.BlockSpec((c, 256), lambda n: (n, OFF_RV // 256)),
            pl.BlockSpec((c, 256), lambda n: (n, OFF_RV // 256 + 1)),
            pl.BlockSpec((c, 256), lambda n: (n, OFF_RG // 256)),
            pl.BlockSpec((c, 256), lambda n: (n, OFF_RG // 256 + 1)),
            pl.BlockSpec((c, 256), lambda n: (n, 0)),
            pl.BlockSpec((c, 256), lambda n: (n, 0)),
            _const_spec((256, 256), (0, 0)),
            _const_spec((RET_HEADS, c, c), (0, 0, 0)),
            _const_spec((c, 256), (0, 0)),
            _const_spec((c, 256), (0, 0)),
            _const_spec((RET_HEADS, 128, RET_DV), (0, 0, 0)),
            _const_spec((None, 1, RET_V_WIDTH), (l, 0, 0)),
        ],
        out_specs=pl.BlockSpec((c, RET_V_WIDTH), lambda n: (n, 0)),
        out_shape=jax.ShapeDtypeStruct((s, RET_V_WIDTH), BF16),
        scratch_shapes=[pltpu.VMEM((RET_HEADS, 128, RET_DV), F32)],
        compiler_params=_params(("arbitrary",)),
        name="retention",
    )(pm, pm, pm, pm, pm, pm, cos_t, sin_t, perm, dec, zeta, xi, cd, ret_nw3)


MG_TM = 512


def _merge_body(x_ref, o0, o1, o2, l0, l1, l2, yb_ref, yc_ref, gi_ref, bg_ref,
                wa_ref, wb_ref, wc_ref, wo_ref, out_ref):
    a0, a1, a2 = l0[...], l1[...], l2[...]
    m = jnp.maximum(jnp.maximum(a0, a1), a2)
    e0, e1, e2 = jnp.exp(a0 - m), jnp.exp(a1 - m), jnp.exp(a2 - m)
    ya = (e0 * o0[...] + e1 * o1[...] + e2 * o2[...]) / (e0 + e1 + e2)
    pa = jnp.dot(ya.astype(BF16), wa_ref[...], preferred_element_type=F32)
    pb = jnp.dot(yb_ref[...], wb_ref[...], preferred_element_type=F32)
    pc = jnp.dot(yc_ref[...], wc_ref[...], preferred_element_type=F32)
    merged = None
    for b, pr in enumerate((pa, pb, pc)):
        gate = jax.nn.sigmoid(gi_ref[:, b * D_MODEL:(b + 1) * D_MODEL].astype(F32) + bg_ref[b:b + 1, :])
        merged = gate * pr if merged is None else merged + gate * pr
    out_ref[...] = x_ref[...] + jnp.dot(merged.astype(BF16), wo_ref[...], preferred_element_type=F32)


def _merge(x, oas, lses, yb, yc, gi, b_gate, wa, wb, wc, wo, l):
    s = x.shape[0]
    tm = MG_TM
    row = lambda w: pl.BlockSpec((tm, w), lambda i: (i, 0))
    return pl.pallas_call(
        _merge_body,
        grid=(s // tm,),
        in_specs=[row(D_MODEL)] + [row(A_OUT)] * 6 + [row(SB_WIDTH), row(RET_V_WIDTH), row(GATE_W),
                  _const_spec((None, N_BRANCH, D_MODEL), (l, 0, 0)),
                  _const_spec((None, A_OUT, D_MODEL), (l, 0, 0)),
                  _const_spec((None, SB_WIDTH, D_MODEL), (l, 0, 0)),
                  _const_spec((None, RET_V_WIDTH, D_MODEL), (l, 0, 0)),
                  _const_spec((None, D_MODEL, D_MODEL), (l, 0, 0))],
        out_specs=row(D_MODEL),
        out_shape=jax.ShapeDtypeStruct((s, D_MODEL), F32),
        compiler_params=_params(("parallel",)),
        name="merge_outproj",
    )(x, *oas, *lses, yb, yc, gi, b_gate, wa, wb, wc, wo)


FF_TM = 512
FF_CH = 256
CARRY = 8


def _gelu_tanh(x):
    return 0.5 * x * (1.0 + jnp.tanh(np.sqrt(2.0 / np.pi).astype(np.float32) * (x + 0.044715 * (x * x * x))))


def _ffn_body(x_ref, nw_ref, wu_ref, wg_ref, cw_ref, cb_ref, wd_ref, out_ref, u_ref, a_ref):
    i = pl.program_id(0)

    @pl.when(i == 0)
    def _():
        u_ref[0:CARRY, :] = jnp.zeros((CARRY, D_FF), F32)

    x = x_ref[...]
    h = (x * lax.rsqrt(jnp.mean(x * x, axis=-1, keepdims=True) + EPS) * nw_ref[...]).astype(BF16)
    for c in range(D_FF // FF_CH):
        sl = slice(c * FF_CH, (c + 1) * FF_CH)
        u_ref[CARRY:CARRY + FF_TM, sl] = jnp.dot(h, wu_ref[:, sl], preferred_element_type=F32)
        u0 = u_ref[CARRY:CARRY + FF_TM, sl]
        u1 = u_ref[CARRY - 1:CARRY - 1 + FF_TM, sl]
        u2 = u_ref[CARRY - 2:CARRY - 2 + FF_TM, sl]
        uc = cw_ref[0:1, sl] * u2 + cw_ref[1:2, sl] * u1 + cw_ref[2:3, sl] * u0 + cb_ref[:, sl]
        g = jnp.dot(h, wg_ref[:, sl], preferred_element_type=F32)
        a_ref[:, sl] = (_gelu_tanh(uc) * g).astype(BF16)
    u_ref[0:CARRY, :] = u_ref[FF_TM:FF_TM + CARRY, :]
    out_ref[...] = x + jnp.dot(a_ref[...], wd_ref[...], preferred_element_type=F32)


def _conv_ffn(x, nw3, wu, wg, cw, cb3, wd, l):
    s = x.shape[0]
    tm = FF_TM
    return pl.pallas_call(
        _ffn_body,
        grid=(s // tm,),
        in_specs=[
            pl.BlockSpec((tm, D_MODEL), lambda i: (i, 0)),
            _const_spec((None, 1, D_MODEL), (l, 0, 0)),
            _const_spec((None, D_MODEL, D_FF), (l, 0, 0)),
            _const_spec((None, D_MODEL, D_FF), (l, 0, 0)),
            _const_spec((None, 3, D_FF), (l, 0, 0)),
            _const_spec((None, 1, D_FF), (l, 0, 0)),
            _const_spec((None, D_FF, D_MODEL), (l, 0, 0)),
        ],
        out_specs=pl.BlockSpec((tm, D_MODEL), lambda i: (i, 0)),
        out_shape=jax.ShapeDtypeStruct((s, D_MODEL), F32),
        scratch_shapes=[pltpu.VMEM((tm + CARRY, D_FF), F32), pltpu.VMEM((tm, D_FF), BF16)],
        compiler_params=_params(("arbitrary",)),
        name="conv_ffn",
    )(x, nw3, wu, wg, cw, cb3, wd)


def _norm_body(x_ref, w_ref, o_ref):
    x = x_ref[...]
    o_ref[...] = x * lax.rsqrt(jnp.mean(x * x, axis=-1, keepdims=True) + EPS) * w_ref[...]


def _final_norm(x, w):
    s = x.shape[0]
    tm = 1024
    return pl.pallas_call(
        _norm_body,
        grid=(s // tm,),
        in_specs=[pl.BlockSpec((tm, D_MODEL), lambda i: (i, 0)), _const_spec((1, D_MODEL), (0, 0))],
        out_specs=pl.BlockSpec((tm, D_MODEL), lambda i: (i, 0)),
        out_shape=jax.ShapeDtypeStruct((s, D_MODEL), F32),
        compiler_params=_params(("parallel",)),
        name="final_norm",
    )(x, w)


def _t5_causal_bucket(dist):
    max_exact = REL_BUCKETS // 2
    n = np.maximum(dist, 0)
    large = max_exact + (np.log(np.maximum(n, 1) / max_exact) / np.log(REL_MAX_DIST / max_exact)
                         * (REL_BUCKETS - max_exact)).astype(np.int64)
    large = np.minimum(large, REL_BUCKETS - 1)
    return np.where(n < max_exact, n, large).astype(np.int32)


def _dilated_bias(rel_bias, g, window, dilation):
    n_steps = window // dilation
    steps = np.arange(BLOCK)[:, None] + BLOCK - np.arange(2 * BLOCK)[None, :]
    in_band = (steps >= 0) & (steps <= n_steps)
    bucket = _t5_causal_bucket(steps * dilation)
    bias_g = rel_bias[:, g * A_HEADS_PER_GROUP:(g + 1) * A_HEADS_PER_GROUP].astype(F32)
    bias = jnp.take(bias_g, jnp.asarray(bucket), axis=0).transpose(2, 0, 1)
    return jnp.where(jnp.asarray(in_band)[None], bias, NEG)


def _retention_tables(s):
    half = RET_DK // 2
    inv = ROPE_BASE ** (-jnp.arange(half, dtype=F32) / half)
    ang = jnp.arange(s, dtype=jnp.int32).astype(F32)[:, None] * inv[None, :]
    cos, sin = jnp.cos(ang), jnp.sin(ang)
    cos_t = jnp.tile(jnp.concatenate([cos, cos], axis=-1), (1, RET_HEADS))
    sin_t = jnp.tile(jnp.concatenate([-sin, sin], axis=-1), (1, RET_HEADS))
    j = np.arange(RET_QK_WIDTH)
    partner = np.where((j % RET_DK) < half, j + half, j - half)
    perm = np.zeros((RET_QK_WIDTH, RET_QK_WIDTH), np.float32)
    perm[partner, j] = 1.0
    log_gamma = jnp.log1p(-jnp.exp2(-5.0 - jnp.arange(RET_HEADS, dtype=F32)))
    n = jnp.arange(RET_CHUNK, dtype=F32)
    diff = n[:, None] - n[None, :]
    dec = jnp.where(diff >= 0, jnp.exp(diff[None] * log_gamma[:, None, None]), 0.0)
    zeta = jnp.exp((RET_CHUNK - 1 - n)[:, None] * log_gamma[None, :])
    xi = jnp.exp((n + 1)[:, None] * log_gamma[None, :])
    zeta = jnp.repeat(zeta, RET_DK, axis=1)
    xi = jnp.repeat(xi, RET_DK, axis=1)
    cd = jnp.broadcast_to(jnp.exp(RET_CHUNK * log_gamma)[:, None, None], (RET_HEADS, 128, RET_DV))
    return cos_t, sin_t, jnp.asarray(perm, BF16), dec, zeta, xi, cd


def _scale_row():
    sc = np.ones((1, MAIN_W), np.float32)
    sc[:, OFF_AQ:OFF_AQ + A_WIDTH] = HEAD_DIM ** -0.5
    sc[:, OFF_SQ:OFF_SQ + SB_WIDTH] = SB_HEAD_DIM ** -0.5
    sc[:, OFF_RK:OFF_RK + RET_QK_WIDTH] = RET_DK ** -0.5
    return jnp.asarray(sc)


def _tri_table():
    k = np.arange(BLOCK)[:, None]
    j = np.arange(BLOCK)[None, :]
    t = np.concatenate([-(k > j).astype(np.float32), -np.ones((BLOCK, BLOCK), np.float32)], axis=1)
    return jnp.asarray(t, BF16)


def kernel(x, rel_bias, norm_mix_w, w_in, b_gate, ret_norm_w, w_proj_a, w_proj_b, w_proj_c, w_out,
           norm_ffn_w, w_up, w_gate, conv_w, conv_b, w_down, final_norm_w):
    b, s, _ = x.shape
    assert b == 1
    xs = x.reshape(s, D_MODEL)
    w_in_bf = w_in.astype(BF16)
    wa, wb, wc, wo = (w.astype(BF16) for w in (w_proj_a, w_proj_b, w_proj_c, w_out))
    wu, wg, wd = (w.astype(BF16) for w in (w_up, w_gate, w_down))
    nmix3 = norm_mix_w.reshape(DEPTH, 1, D_MODEL)
    nffn3 = norm_ffn_w.reshape(DEPTH, 1, D_MODEL)
    rnw3 = ret_norm_w.reshape(DEPTH, 1, RET_V_WIDTH)
    cb3 = conv_b.reshape(DEPTH, 1, D_FF)
    scale_row = _scale_row()
    tri = _tri_table()
    biases = [_dilated_bias(rel_bias, g, w, d) for g, (w, d) in enumerate(DIL_GROUPS)]
    rtabs = _retention_tables(s)

    for l in range(DEPTH):
        pm, gi = _inproj(xs, nmix3, scale_row, w_in_bf, l)
        oas, lses = [], []
        for g, (_, dilation) in enumerate(DIL_GROUPS):
            o_g, lse_g = _dilated_group(pm, biases[g], g, dilation)
            oas.append(o_g)
            lses.append(lse_g)
        yb = _stick_breaking(pm, tri)
        yc = _retention(pm, rtabs, rnw3, l)
        xs = _merge(xs, oas, lses, yb, yc, gi, b_gate, wa, wb, wc, wo, l)
        xs = _conv_ffn(xs, nffn3, wu, wg, conv_w, cb3, wd, l)
    return _final_norm(xs, final_norm_w.reshape(1, D_MODEL)).reshape(b, s, D_MODEL)
```

```python
import functools

import numpy as np
import jax
import jax.numpy as jnp
from jax import lax
from jax.experimental import pallas as pl
from jax.experimental.pallas import tpu as pltpu

F32 = jnp.float32
BF16 = jnp.bfloat16

D_MODEL = 1024
DEPTH = 4
HEAD_DIM = 64
BLOCK = 128
DIL_GROUPS = ((128, 1), (512, 4), (2048, 16))
N_GROUPS = 3
A_HEADS_PER_GROUP = 4
A_WIDTH = 768
A_OUT = 256
SB_HEADS = 4
SB_HEAD_DIM = 128
SB_WIDTH = 512
RET_HEADS = 4
RET_DK = 64
RET_DV = 128
RET_QK_WIDTH = 256
RET_V_WIDTH = 512
RET_CHUNK = 128
ROPE_BASE = 10000.0
REL_BUCKETS = 32
REL_MAX_DIST = 2048
N_BRANCH = 3
D_FF = 2816
EPS = 1e-6

GATE_W = N_BRANCH * D_MODEL
MAIN_W = 3 * A_WIDTH + 3 * SB_WIDTH + 2 * RET_QK_WIDTH + 2 * RET_V_WIDTH
IN_WIDTH = MAIN_W + GATE_W
OFF_AQ, OFF_AK, OFF_AV = 0, 768, 1536
OFF_SQ, OFF_SK, OFF_SV = 2304, 2816, 3328
OFF_RQ, OFF_RK, OFF_RV, OFF_RG = 3840, 4096, 4352, 4864

NEG = -1e30
VMEM_LIMIT = 56 * 1024 * 1024

NT_DIMS = (((1,), (1,)), ((), ()))
TN_DIMS = (((0,), (0,)), ((), ()))


def _params(sem, vmem=VMEM_LIMIT):
    return pltpu.CompilerParams(dimension_semantics=sem, vmem_limit_bytes=vmem)


def _const_spec(shape, index):
    nd = len(index)
    return pl.BlockSpec(shape, lambda *_: index, pipeline_mode=pl.Buffered(1))


IN_TM = 512
IN_CH = 768


def _inproj_body(x_ref, nw_ref, sc_ref, w_ref, pm_ref, gi_ref):
    x = x_ref[...]
    h = (x * lax.rsqrt(jnp.mean(x * x, axis=-1, keepdims=True) + EPS) * nw_ref[...]).astype(BF16)
    for c in range(MAIN_W // IN_CH):
        sl = slice(c * IN_CH, (c + 1) * IN_CH)
        y = jnp.dot(h, w_ref[:, sl], preferred_element_type=F32)
        pm_ref[:, sl] = (y * sc_ref[:, sl]).astype(BF16)
    for c in range(GATE_W // IN_CH):
        y = jnp.dot(h, w_ref[:, MAIN_W + c * IN_CH:MAIN_W + (c + 1) * IN_CH], preferred_element_type=F32)
        gi_ref[:, c * IN_CH:(c + 1) * IN_CH] = y.astype(BF16)


def _inproj(x, norm_w3, scale_row, w_in_bf, l):
    s = x.shape[0]
    return pl.pallas_call(
        _inproj_body,
        grid=(s // IN_TM,),
        in_specs=[
            pl.BlockSpec((IN_TM, D_MODEL), lambda i: (i, 0)),
            _const_spec((None, 1, D_MODEL), (l, 0, 0)),
            _const_spec((1, MAIN_W), (0, 0)),
            _const_spec((None, D_MODEL, IN_WIDTH), (l, 0, 0)),
        ],
        out_specs=[
            pl.BlockSpec((IN_TM, MAIN_W), lambda i: (i, 0)),
            pl.BlockSpec((IN_TM, GATE_W), lambda i: (i, 0)),
        ],
        out_shape=[jax.ShapeDtypeStruct((s, MAIN_W), BF16), jax.ShapeDtypeStruct((s, GATE_W), BF16)],
        compiler_params=_params(("parallel",)),
        name="inproj",
    )(x, norm_w3, scale_row, w_in_bf)


def _dil_body(q_ref, kc_ref, kp_ref, vc_ref, vp_ref, bias_ref, o_ref, lse_ref):
    n = pl.program_id(1)
    lane = lax.broadcasted_iota(jnp.int32, (BLOCK, 128), 1)
    lo = lane < HEAD_DIM
    prev_ok = n > 0
    for p in range(2):
        sl = slice(p * 128, (p + 1) * 128)
        q = q_ref[:, sl]
        kc, kp, vc, vp = kc_ref[:, sl], kp_ref[:, sl], vc_ref[:, sl], vp_ref[:, sl]
        outs, lses = [], []
        for hh in range(2):
            h = 2 * p + hh
            sel = lo if hh == 0 else jnp.logical_not(lo)
            qm = jnp.where(sel, q, jnp.zeros_like(q))
            s_cur = lax.dot_general(qm, kc, NT_DIMS, preferred_element_type=F32) + bias_ref[h, :, BLOCK:]
            s_prev = lax.dot_general(qm, kp, NT_DIMS, preferred_element_type=F32) + bias_ref[h, :, :BLOCK]
            s_prev = jnp.where(prev_ok, s_prev, NEG)
            m = jnp.maximum(jnp.max(s_cur, axis=-1, keepdims=True), jnp.max(s_prev, axis=-1, keepdims=True))
            p_cur = jnp.exp(s_cur - m)
            p_prev = jnp.exp(s_prev - m)
            den = jnp.sum(p_cur, axis=-1, keepdims=True) + jnp.sum(p_prev, axis=-1, keepdims=True)
            num = (jnp.dot(p_cur.astype(BF16), vc, preferred_element_type=F32)
                   + jnp.dot(p_prev.astype(BF16), vp, preferred_element_type=F32))
            outs.append(num / den)
            lses.append(jnp.broadcast_to(m + jnp.log(den), (BLOCK, 128)))
        o_ref[:, sl] = jnp.where(lo, outs[0], outs[1])
        lse_ref[:, sl] = jnp.where(lo, lses[0], lses[1])


def _dilated_group(pm, bias_g, g, dilation):
    s = pm.shape[0]
    rows = s // dilation
    nb = rows // BLOCK
    pmv = pm.reshape(rows, dilation * MAIN_W)
    cb = MAIN_W // 256
    qb, kb, vb = OFF_AQ // 256 + g, OFF_AK // 256 + g, OFF_AV // 256 + g

    def cur(off):
        return pl.BlockSpec((BLOCK, 256), lambda r, n: (n, r * cb + off))

    def prev(off):
        return pl.BlockSpec((BLOCK, 256), lambda r, n: (jnp.maximum(n - 1, 0), r * cb + off))

    o, lse = pl.pallas_call(
        _dil_body,
        grid=(dilation, nb),
        in_specs=[cur(qb), cur(kb), prev(kb), cur(vb), prev(vb),
                  _const_spec((A_HEADS_PER_GROUP, BLOCK, 2 * BLOCK), (0, 0, 0))],
        out_specs=[pl.BlockSpec((BLOCK, 256), lambda r, n: (n, r)),
                   pl.BlockSpec((BLOCK, 256), lambda r, n: (n, r))],
        out_shape=[jax.ShapeDtypeStruct((rows, dilation * A_OUT), F32),
                   jax.ShapeDtypeStruct((rows, dilation * A_OUT), F32)],
        compiler_params=_params(("parallel", "parallel")),
        name=f"dilated_attn_d{dilation}",
    )(pmv, pmv, pmv, pmv, pmv, bias_g)
    return o.reshape(s, A_OUT), lse.reshape(s, A_OUT)


SB_TQ = 512
SB_TK = 256
LOG2E = 1.4426950408889634


def _sb_body(q_ref, k_ref, v_ref, tri_ref, o_ref, acc_ref, c_ref, sp_ref, u0_ref, u_ref):
    i = pl.program_id(1)
    qi = i * SB_TQ
    acc_ref[...] = jnp.zeros_like(acc_ref)
    c_ref[...] = jnp.zeros_like(c_ref)
    row = lax.broadcasted_iota(jnp.int32, (SB_TQ, SB_TK), 0)
    col = lax.broadcasted_iota(jnp.int32, (SB_TQ, SB_TK), 1)

    def key_start(t):
        return pl.multiple_of(qi + SB_TK - t * SB_TK, SB_TK)

    def stage_a(t, slot, masked):
        kb = k_ref[pl.ds(key_start(t), SB_TK), :]
        z = lax.dot_general(q_ref[...], kb, NT_DIMS, preferred_element_type=F32)
        sp = jnp.maximum(z, 0.0) + jnp.log2(1.0 + jnp.exp2(-jnp.abs(z)))
        c = c_ref[...]
        u0 = z - sp + jnp.concatenate([c, c], axis=1)
        if masked:
            mask = (col + (SB_TK - t * SB_TK)) < row
            sp = jnp.where(mask, sp, 0.0)
            u0 = jnp.where(mask, u0, NEG)
        sp_ref[slot] = sp.astype(BF16)
        u0_ref[slot] = u0
        c_ref[...] = c - jnp.sum(sp, axis=-1, keepdims=True)

    def stage_s2(slot):
        u_ref[slot] = u0_ref[slot] + jnp.dot(sp_ref[slot], tri_ref[...], preferred_element_type=F32)

    def stage_s3(t, slot):
        vb = v_ref[pl.ds(key_start(t), SB_TK), :]
        w = jnp.exp2(u_ref[slot]).astype(BF16)
        acc_ref[...] += jnp.dot(w, vb, preferred_element_type=F32)

    n = 2 * i + SB_TQ // SB_TK
    stage_a(0, 0, True)
    stage_s2(0)
    stage_a(1, 1, True)

    def body(p, carry):
        it = 2 * p + 2
        stage_s3(it - 2, 0)
        stage_s2(1)
        stage_a(it, 0, False)
        stage_s3(it - 1, 1)
        stage_s2(0)
        stage_a(it + 1, 1, False)
        return carry

    lax.fori_loop(0, i, body, 0)
    stage_s3(n - 2, 0)
    stage_s2(1)
    stage_s3(n - 1, 1)
    o_ref[...] = acc_ref[...].astype(o_ref.dtype)


def _stick_breaking(pm, tri):
    s = pm.shape[0]
    qb, kb, vb = OFF_SQ // 128, OFF_SK // 128, OFF_SV // 128
    return pl.pallas_call(
        _sb_body,
        grid=(SB_HEADS, s // SB_TQ),
        in_specs=[
            pl.BlockSpec((SB_TQ, 128), lambda h, i: (i, qb + h)),
            pl.BlockSpec((s, 128), lambda h, i: (0, kb + h)),
            pl.BlockSpec((s, 128), lambda h, i: (0, vb + h)),
            _const_spec((SB_TK, SB_TK), (0, 0)),
        ],
        out_specs=pl.BlockSpec((SB_TQ, 128), lambda h, i: (i, h)),
        out_shape=jax.ShapeDtypeStruct((s, SB_WIDTH), BF16),
        scratch_shapes=[pltpu.VMEM((SB_TQ, 128), F32), pltpu.VMEM((SB_TQ, 128), F32),
                        pltpu.VMEM((2, SB_TQ, SB_TK), BF16), pltpu.VMEM((2, SB_TQ, SB_TK), F32),
                        pltpu.VMEM((2, SB_TQ, SB_TK), F32)],
        compiler_params=_params(("parallel", "parallel")),
        name="stick_breaking",
    )(pm, pm, pm, tri)


def _ret_body(q_ref, k_ref, v01_ref, v23_ref, g01_ref, g23_ref, cos_ref, sin_ref, perm_ref, dec_ref,
              zeta_ref, xi_ref, cd_ref, nw_ref, o_ref, r_ref):
    n = pl.program_id(0)
    v_refs = (v01_ref, v23_ref)
    g_refs = (g01_ref, g23_ref)

    @pl.when(n == 0)
    def _():
        r_ref[...] = jnp.zeros_like(r_ref)

    cos = cos_ref[...]
    sin = sin_ref[...]
    perm = perm_ref[...]

    def rot(x):
        swapped = jnp.dot(x, perm, preferred_element_type=F32)
        return x.astype(F32) * cos + swapped * sin

    rq = rot(q_ref[...])
    rk = rot(k_ref[...])
    qb = rq.astype(BF16)
    qx = (rq * xi_ref[...]).astype(BF16)
    kb = rk.astype(BF16)
    kz = (rk * zeta_ref[...]).astype(BF16)
    lane = lax.broadcasted_iota(jnp.int32, (RET_CHUNK, 128), 1)
    lo = lane < RET_DK
    zero = jnp.zeros((RET_CHUNK, 128), BF16)
    for h in range(RET_HEADS):
        sl = slice((h // 2) * 128, (h // 2 + 1) * 128)
        sel = lo if h % 2 == 0 else jnp.logical_not(lo)
        hs = slice((h % 2) * RET_DV, (h % 2 + 1) * RET_DV)
        v = v_refs[h // 2][:, hs]
        qm = jnp.where(sel, qb[:, sl], zero)
        qxm = jnp.where(sel, qx[:, sl], zero)
        intra = lax.dot_general(qm, kb[:, sl], NT_DIMS, preferred_element_type=F32) * dec_ref[h]
        o = jnp.dot(intra.astype(BF16), v, preferred_element_type=F32)
        o = o + jnp.dot(qxm, r_ref[h].astype(BF16), preferred_element_type=F32)
        kv = lax.dot_general(kz[:, sl], v, TN_DIMS, preferred_element_type=F32)
        r_ref[h] = r_ref[h] * cd_ref[h] + kv
        mu = jnp.mean(o, axis=-1, keepdims=True)
        d = o - mu
        var = jnp.mean(d * d, axis=-1, keepdims=True)
        on = (d * lax.rsqrt(var + EPS)) * nw_ref[:, h * RET_DV:(h + 1) * RET_DV]
        gate = g_refs[h // 2][:, hs].astype(F32)
        o_ref[:, h * RET_DV:(h + 1) * RET_DV] = (gate * jax.nn.sigmoid(gate) * on).astype(o_ref.dtype)


def _retention(pm, tabs, ret_nw3, l):
    s = pm.shape[0]
    cos_t, sin_t, perm, dec, zeta, xi, cd = tabs
    c = RET_CHUNK
    return pl.pallas_call(
        _ret_body,
        grid=(s // c,),
        in_specs=[
            pl.BlockSpec((c, 256), lambda n: (n, OFF_RQ // 256)),
            pl.BlockSpec((c, 256), lambda n: (n, OFF_RK // 256)),
            pl.BlockSpec((c, 256), lambda n: (n, OFF_RV // 256)),
            pl.BlockSpec((c, 256), lambda n: (n, OFF_RV // 256 + 1)),
            pl.BlockSpec((c, 256), lambda n: (n, OFF_RG // 256)),
            pl.BlockSpec((c, 256), lambda n: (n, OFF_RG // 256 + 1)),
            pl.BlockSpec((c, 256), lambda n: (n, 0)),
            pl.BlockSpec((c, 256), lambda n: (n, 0)),
            _const_spec((256, 256), (0, 0)),
            _const_spec((RET_HEADS, c, c), (0, 0, 0)),
            _const_spec((c, 256), (0, 0)),
            _const_spec((c, 256), (0, 0)),
            _const_spec((RET_HEADS, 128, RET_DV), (0, 0, 0)),
            _const_spec((None, 1, RET_V_WIDTH), (l, 0, 0)),
        ],
        out_specs=pl.BlockSpec((c, RET_V_WIDTH), lambda n: (n, 0)),
        out_shape=jax.ShapeDtypeStruct((s, RET_V_WIDTH), BF16),
        scratch_shapes=[pltpu.VMEM((RET_HEADS, 128, RET_DV), F32)],
        compiler_params=_params(("arbitrary",)),
        name="retention",
    )(pm, pm, pm, pm, pm, pm, cos_t, sin_t, perm, dec, zeta, xi, cd, ret_nw3)


MG_TM = 512


def _merge_body(x_ref, o0, o1, o2, l0, l1, l2, yb_ref, yc_ref, gi_ref, bg_ref,
                wa_ref, wb_ref, wc_ref, wo_ref, out_ref):
    a0, a1, a2 = l0[...], l1[...], l2[...]
    m = jnp.maximum(jnp.maximum(a0, a1), a2)
    e0, e1, e2 = jnp.exp(a0 - m), jnp.exp(a1 - m), jnp.exp(a2 - m)
    ya = (e0 * o0[...] + e1 * o1[...] + e2 * o2[...]) / (e0 + e1 + e2)
    pa = jnp.dot(ya.astype(BF16), wa_ref[...], preferred_element_type=F32)
    pb = jnp.dot(yb_ref[...], wb_ref[...], preferred_element_type=F32)
    pc = jnp.dot(yc_ref[...], wc_ref[...], preferred_element_type=F32)
    merged = None
    for b, pr in enumerate((pa, pb, pc)):
        gate = jax.nn.sigmoid(gi_ref[:, b * D_MODEL:(b + 1) * D_MODEL].astype(F32) + bg_ref[b:b + 1, :])
        merged = gate * pr if merged is None else merged + gate * pr
    out_ref[...] = x_ref[...] + jnp.dot(merged.astype(BF16), wo_ref[...], preferred_element_type=F32)


def _merge(x, oas, lses, yb, yc, gi, b_gate, wa, wb, wc, wo, l):
    s = x.shape[0]
    tm = MG_TM
    row = lambda w: pl.BlockSpec((tm, w), lambda i: (i, 0))
    return pl.pallas_call(
        _merge_body,
        grid=(s // tm,),
        in_specs=[row(D_MODEL)] + [row(A_OUT)] * 6 + [row(SB_WIDTH), row(RET_V_WIDTH), row(GATE_W),
                  _const_spec((None, N_BRANCH, D_MODEL), (l, 0, 0)),
                  _const_spec((None, A_OUT, D_MODEL), (l, 0, 0)),
                  _const_spec((None, SB_WIDTH, D_MODEL), (l, 0, 0)),
                  _const_spec((None, RET_V_WIDTH, D_MODEL), (l, 0, 0)),
                  _const_spec((None, D_MODEL, D_MODEL), (l, 0, 0))],
        out_specs=row(D_MODEL),
        out_shape=jax.ShapeDtypeStruct((s, D_MODEL), F32),
        compiler_params=_params(("parallel",)),
        name="merge_outproj",
    )(x, *oas, *lses, yb, yc, gi, b_gate, wa, wb, wc, wo)


FF_TM = 512
FF_CH = 256
CARRY = 8


def _gelu_tanh(x):
    return 0.5 * x * (1.0 + jnp.tanh(np.sqrt(2.0 / np.pi).astype(np.float32) * (x + 0.044715 * (x * x * x))))


def _ffn_body(x_ref, nw_ref, wu_ref, wg_ref, cw_ref, cb_ref, wd_ref, out_ref, u_ref, a_ref):
    i = pl.program_id(0)

    @pl.when(i == 0)
    def _():
        u_ref[0:CARRY, :] = jnp.zeros((CARRY, D_FF), F32)

    x = x_ref[...]
    h = (x * lax.rsqrt(jnp.mean(x * x, axis=-1, keepdims=True) + EPS) * nw_ref[...]).astype(BF16)
    for c in range(D_FF // FF_CH):
        sl = slice(c * FF_CH, (c + 1) * FF_CH)
        u_ref[CARRY:CARRY + FF_TM, sl] = jnp.dot(h, wu_ref[:, sl], preferred_element_type=F32)
        u0 = u_ref[CARRY:CARRY + FF_TM, sl]
        u1 = u_ref[CARRY - 1:CARRY - 1 + FF_TM, sl]
        u2 = u_ref[CARRY - 2:CARRY - 2 + FF_TM, sl]
        uc = cw_ref[0:1, sl] * u2 + cw_ref[1:2, sl] * u1 + cw_ref[2:3, sl] * u0 + cb_ref[:, sl]
        g = jnp.dot(h, wg_ref[:, sl], preferred_element_type=F32)
        a_ref[:, sl] = (_gelu_tanh(uc) * g).astype(BF16)
    u_ref[0:CARRY, :] = u_ref[FF_TM:FF_TM + CARRY, :]
    out_ref[...] = x + jnp.dot(a_ref[...], wd_ref[...], preferred_element_type=F32)


def _conv_ffn(x, nw3, wu, wg, cw, cb3, wd, l):
    s = x.shape[0]
    tm = FF_TM
    return pl.pallas_call(
        _ffn_body,
        grid=(s // tm,),
        in_specs=[
            pl.BlockSpec((tm, D_MODEL), lambda i: (i, 0)),
            _const_spec((None, 1, D_MODEL), (l, 0, 0)),
            _const_spec((None, D_MODEL, D_FF), (l, 0, 0)),
            _const_spec((None, D_MODEL, D_FF), (l, 0, 0)),
            _const_spec((None, 3, D_FF), (l, 0, 0)),
            _const_spec((None, 1, D_FF), (l, 0, 0)),
            _const_spec((None, D_FF, D_MODEL), (l, 0, 0)),
        ],
        out_specs=pl.BlockSpec((tm, D_MODEL), lambda i: (i, 0)),
        out_shape=jax.ShapeDtypeStruct((s, D_MODEL), F32),
        scratch_shapes=[pltpu.VMEM((tm + CARRY, D_FF), F32), pltpu.VMEM((tm, D_FF), BF16)],
        compiler_params=_params(("arbitrary",)),
        name="conv_ffn",
    )(x, nw3, wu, wg, cw, cb3, wd)


def _norm_body(x_ref, w_ref, o_ref):
    x = x_ref[...]
    o_ref[...] = x * lax.rsqrt(jnp.mean(x * x, axis=-1, keepdims=True) + EPS) * w_ref[...]


def _final_norm(x, w):
    s = x.shape[0]
    tm = 1024
    return pl.pallas_call(
        _norm_body,
        grid=(s // tm,),
        in_specs=[pl.BlockSpec((tm, D_MODEL), lambda i: (i, 0)), _const_spec((1, D_MODEL), (0, 0))],
        out_specs=pl.BlockSpec((tm, D_MODEL), lambda i: (i, 0)),
        out_shape=jax.ShapeDtypeStruct((s, D_MODEL), F32),
        compiler_params=_params(("parallel",)),
        name="final_norm",
    )(x, w)


def _t5_causal_bucket(dist):
    max_exact = REL_BUCKETS // 2
    n = np.maximum(dist, 0)
    large = max_exact + (np.log(np.maximum(n, 1) / max_exact) / np.log(REL_MAX_DIST / max_exact)
                         * (REL_BUCKETS - max_exact)).astype(np.int64)
    large = np.minimum(large, REL_BUCKETS - 1)
    return np.where(n < max_exact, n, large).astype(np.int32)


def _dilated_bias(rel_bias, g, window, dilation):
    n_steps = window // dilation
    steps = np.arange(BLOCK)[:, None] + BLOCK - np.arange(2 * BLOCK)[None, :]
    in_band = (steps >= 0) & (steps <= n_steps)
    bucket = _t5_causal_bucket(steps * dilation)
    bias_g = rel_bias[:, g * A_HEADS_PER_GROUP:(g + 1) * A_HEADS_PER_GROUP].astype(F32)
    bias = jnp.take(bias_g, jnp.asarray(bucket), axis=0).transpose(2, 0, 1)
    return jnp.where(jnp.asarray(in_band)[None], bias, NEG)


def _retention_tables(s):
    half = RET_DK // 2
    inv = ROPE_BASE ** (-jnp.arange(half, dtype=F32) / half)
    ang = jnp.arange(s, dtype=jnp.int32).astype(F32)[:, None] * inv[None, :]
    cos, sin = jnp.cos(ang), jnp.sin(ang)
    cos_t = jnp.tile(jnp.concatenate([cos, cos], axis=-1), (1, RET_HEADS))
    sin_t = jnp.tile(jnp.concatenate([-sin, sin], axis=-1), (1, RET_HEADS))
    j = np.arange(RET_QK_WIDTH)
    partner = np.where((j % RET_DK) < half, j + half, j - half)
    perm = np.zeros((RET_QK_WIDTH, RET_QK_WIDTH), np.float32)
    perm[partner, j] = 1.0
    log_gamma = jnp.log1p(-jnp.exp2(-5.0 - jnp.arange(RET_HEADS, dtype=F32)))
    n = jnp.arange(RET_CHUNK, dtype=F32)
    diff = n[:, None] - n[None, :]
    dec = jnp.where(diff >= 0, jnp.exp(diff[None] * log_gamma[:, None, None]), 0.0)
    zeta = jnp.exp((RET_CHUNK - 1 - n)[:, None] * log_gamma[None, :])
    xi = jnp.exp((n + 1)[:, None] * log_gamma[None, :])
    zeta = jnp.repeat(zeta, RET_DK, axis=1)
    xi = jnp.repeat(xi, RET_DK, axis=1)
    cd = jnp.broadcast_to(jnp.exp(RET_CHUNK * log_gamma)[:, None, None], (RET_HEADS, 128, RET_DV))
    return cos_t, sin_t, jnp.asarray(perm, BF16), dec, zeta, xi, cd


def _scale_row():
    sc = np.ones((1, MAIN_W), np.float32)
    sc[:, OFF_AQ:OFF_AQ + A_WIDTH] = HEAD_DIM ** -0.5
    sc[:, OFF_SQ:OFF_SQ + SB_WIDTH] = SB_HEAD_DIM ** -0.5 * LOG2E
    sc[:, OFF_RK:OFF_RK + RET_QK_WIDTH] = RET_DK ** -0.5
    return jnp.asarray(sc)


def _tri_table():
    k = np.arange(SB_TK)[:, None]
    j = np.arange(SB_TK)[None, :]
    return jnp.asarray(-(k > j).astype(np.float32), BF16)


def kernel(x, rel_bias, norm_mix_w, w_in, b_gate, ret_norm_w, w_proj_a, w_proj_b, w_proj_c, w_out,
           norm_ffn_w, w_up, w_gate, conv_w, conv_b, w_down, final_norm_w):
    b, s, _ = x.shape
    assert b == 1
    xs = x.reshape(s, D_MODEL)
    w_in_bf = w_in.astype(BF16)
    wa, wb, wc, wo = (w.astype(BF16) for w in (w_proj_a, w_proj_b, w_proj_c, w_out))
    wu, wg, wd = (w.astype(BF16) for w in (w_up, w_gate, w_down))
    nmix3 = norm_mix_w.reshape(DEPTH, 1, D_MODEL)
    nffn3 = norm_ffn_w.reshape(DEPTH, 1, D_MODEL)
    rnw3 = ret_norm_w.reshape(DEPTH, 1, RET_V_WIDTH)
    cb3 = conv_b.reshape(DEPTH, 1, D_FF)
    scale_row = _scale_row()
    tri = _tri_table()
    biases = [_dilated_bias(rel_bias, g, w, d) for g, (w, d) in enumerate(DIL_GROUPS)]
    rtabs = _retention_tables(s)

    for l in range(DEPTH):
        pm, gi = _inproj(xs, nmix3, scale_row, w_in_bf, l)
        oas, lses = [], []
        for g, (_, dilation) in enumerate(DIL_GROUPS):
            o_g, lse_g = _dilated_group(pm, biases[g], g, dilation)
            oas.append(o_g)
            lses.append(lse_g)
        yb = _stick_breaking(pm, tri)
        yc = _retention(pm, rtabs, rnw3, l)
        xs = _merge(xs, oas, lses, yb, yc, gi, b_gate, wa, wb, wc, wo, l)
        xs = _conv_ffn(xs, nffn3, wu, wg, conv_w, cb3, wd, l)
    return _final_norm(xs, final_norm_w.reshape(1, D_MODEL)).reshape(b, s, D_MODEL)
```

```python
import functools

import numpy as np
import jax
import jax.numpy as jnp
from jax import lax
from jax.experimental import pallas as pl
from jax.experimental.pallas import tpu as pltpu

F32 = jnp.float32
BF16 = jnp.bfloat16

D_MODEL = 1024
DEPTH = 4
HEAD_DIM = 64
BLOCK = 128
DIL_GROUPS = ((128, 1), (512, 4), (2048, 16))
N_GROUPS = 3
A_HEADS_PER_GROUP = 4
A_WIDTH = 768
A_OUT = 256
SB_HEADS = 4
SB_HEAD_DIM = 128
SB_WIDTH = 512
RET_HEADS = 4
RET_DK = 64
RET_DV = 128
RET_QK_WIDTH = 256
RET_V_WIDTH = 512
RET_CHUNK = 128
ROPE_BASE = 10000.0
REL_BUCKETS = 32
REL_MAX_DIST = 2048
N_BRANCH = 3
D_FF = 2816
EPS = 1e-6

GATE_W = N_BRANCH * D_MODEL
A_W = 3 * A_WIDTH
PM_W = 3 * SB_WIDTH + 2 * RET_QK_WIDTH + 2 * RET_V_WIDTH
MAIN_W = A_W + PM_W
IN_WIDTH = MAIN_W + GATE_W
G_W = 3 * A_OUT
OFF_SQ, OFF_SK, OFF_SV = 0, 512, 1024
OFF_RQ, OFF_RK, OFF_RV, OFF_RG = 1536, 1792, 2048, 2560

NEG = -1e30
VMEM_LIMIT = 56 * 1024 * 1024

NT_DIMS = (((1,), (1,)), ((), ()))
TN_DIMS = (((0,), (0,)), ((), ()))


def _params(sem, vmem=VMEM_LIMIT):
    return pltpu.CompilerParams(dimension_semantics=sem, vmem_limit_bytes=vmem)


def _const_spec(shape, index):
    nd = len(index)
    return pl.BlockSpec(shape, lambda *_: index, pipeline_mode=pl.Buffered(1))


IN_TM = 512
IN_CH = 768


def _inproj_body(x_ref, nw_ref, sc_ref, w_ref, a0_ref, a1_ref, a2_ref, pm_ref, gi_ref, ys_ref):
    x = x_ref[...]
    h = (x * lax.rsqrt(jnp.mean(x * x, axis=-1, keepdims=True) + EPS) * nw_ref[...]).astype(BF16)
    a_refs = (a0_ref, a1_ref, a2_ref)
    for c in range(A_W // IN_CH):
        sl = slice(c * IN_CH, (c + 1) * IN_CH)
        y = jnp.dot(h, w_ref[:, sl], preferred_element_type=F32) * sc_ref[:, sl]
        for j in range(IN_CH // 128):
            ys_ref[j] = y[:, j * 128:(j + 1) * 128]
        for g, (_, d) in enumerate(DIL_GROUPS):
            for r in range(d):
                rows = pl.ds(r, IN_TM // d, stride=d) if d > 1 else slice(None)
                for j in range(A_OUT // 128):
                    col = r * G_W + c * A_OUT + j * 128
                    a_refs[g][:, col:col + 128] = ys_ref[g * (A_OUT // 128) + j, rows, :].astype(BF16)
    for c in range(PM_W // IN_CH):
        sl = slice(A_W + c * IN_CH, A_W + (c + 1) * IN_CH)
        y = jnp.dot(h, w_ref[:, sl], preferred_element_type=F32)
        pm_ref[:, c * IN_CH:(c + 1) * IN_CH] = (y * sc_ref[:, sl]).astype(BF16)
    for c in range(GATE_W // IN_CH):
        y = jnp.dot(h, w_ref[:, MAIN_W + c * IN_CH:MAIN_W + (c + 1) * IN_CH], preferred_element_type=F32)
        gi_ref[:, c * IN_CH:(c + 1) * IN_CH] = y.astype(BF16)


def _inproj(x, norm_w3, scale_row, w_in_bf, l):
    s = x.shape[0]
    a_shapes = [jax.ShapeDtypeStruct((s // d, d * G_W), BF16) for _, d in DIL_GROUPS]
    a_specs = [pl.BlockSpec((IN_TM // d, d * G_W), lambda i: (i, 0)) for _, d in DIL_GROUPS]
    return pl.pallas_call(
        _inproj_body,
        grid=(s // IN_TM,),
        in_specs=[
            pl.BlockSpec((IN_TM, D_MODEL), lambda i: (i, 0)),
            _const_spec((None, 1, D_MODEL), (l, 0, 0)),
            _const_spec((1, MAIN_W), (0, 0)),
            _const_spec((None, D_MODEL, IN_WIDTH), (l, 0, 0)),
        ],
        out_specs=a_specs + [
            pl.BlockSpec((IN_TM, PM_W), lambda i: (i, 0)),
            pl.BlockSpec((IN_TM, GATE_W), lambda i: (i, 0)),
        ],
        out_shape=a_shapes + [jax.ShapeDtypeStruct((s, PM_W), BF16), jax.ShapeDtypeStruct((s, GATE_W), BF16)],
        scratch_shapes=[pltpu.VMEM((IN_CH // 128, IN_TM, 128), F32)],
        compiler_params=_params(("parallel",)),
        name="inproj",
    )(x, norm_w3, scale_row, w_in_bf)


DIL_NB = 4


def _dil_body(q_ref, k_ref, ke_ref, v_ref, ve_ref, bias_ref, o_ref, lse_ref, *, nblk):
    n = pl.program_id(1)
    lane = lax.broadcasted_iota(jnp.int32, (BLOCK, 128), 1)
    lo = lane < HEAD_DIM
    col = lax.broadcasted_iota(jnp.int32, (BLOCK, 2 * BLOCK), 1)
    first_ok = jnp.logical_or(n > 0, col >= BLOCK)
    for b in range(nblk):
        for p in range(2):
            sl = slice(p * 128, (p + 1) * 128)
            q = q_ref[b * BLOCK:(b + 1) * BLOCK, sl]
            if b == 0:
                k2 = jnp.concatenate([ke_ref[:, sl], k_ref[0:BLOCK, sl]], axis=0)
                v2 = jnp.concatenate([ve_ref[:, sl], v_ref[0:BLOCK, sl]], axis=0)
            else:
                k2 = k_ref[(b - 1) * BLOCK:(b + 1) * BLOCK, sl]
                v2 = v_ref[(b - 1) * BLOCK:(b + 1) * BLOCK, sl]
            outs, lses = [], []
            for hh in range(2):
                h = 2 * p + hh
                sel = lo if hh == 0 else jnp.logical_not(lo)
                qm = jnp.where(sel, q, jnp.zeros_like(q))
                s = lax.dot_general(qm, k2, NT_DIMS, preferred_element_type=F32) + bias_ref[h]
                if b == 0:
                    s = jnp.where(first_ok, s, NEG)
                m = jnp.max(s, axis=-1, keepdims=True)
                e = jnp.exp(s - m)
                den = jnp.sum(e, axis=-1, keepdims=True)
                num = jnp.dot(e.astype(BF16), v2, preferred_element_type=F32)
                outs.append(num / den)
                lses.append(jnp.broadcast_to(m + jnp.log(den), (BLOCK, 128)))
            o_ref[b * BLOCK:(b + 1) * BLOCK, sl] = jnp.where(lo, outs[0], outs[1])
            lse_ref[b * BLOCK:(b + 1) * BLOCK, sl] = jnp.where(lo, lses[0], lses[1])


def _dilated_group(a_g, bias_g, dilation):
    rows = a_g.shape[0]
    nblk = min(DIL_NB, rows // BLOCK)
    tr = nblk * BLOCK
    cb = G_W // A_OUT

    def cur(off):
        return pl.BlockSpec((tr, A_OUT), lambda r, n: (n, r * cb + off))

    def edge(off):
        return pl.BlockSpec((BLOCK, A_OUT), lambda r, n: (jnp.maximum(n * nblk - 1, 0), r * cb + off))

    return pl.pallas_call(
        functools.partial(_dil_body, nblk=nblk),
        grid=(dilation, rows // tr),
        in_specs=[cur(0), cur(1), edge(1), cur(2), edge(2),
                  _const_spec((A_HEADS_PER_GROUP, BLOCK, 2 * BLOCK), (0, 0, 0))],
        out_specs=[pl.BlockSpec((tr, A_OUT), lambda r, n: (n, r)),
                   pl.BlockSpec((tr, A_OUT), lambda r, n: (n, r))],
        out_shape=[jax.ShapeDtypeStruct((rows, dilation * A_OUT), F32),
                   jax.ShapeDtypeStruct((rows, dilation * A_OUT), F32)],
        compiler_params=_params(("parallel", "parallel")),
        name=f"dilated_attn_d{dilation}",
    )(a_g, a_g, a_g, a_g, a_g, bias_g)


SB_TQ = 1024
SB_TK = 256
SB_ND = SB_TQ // SB_TK
LOG2E = 1.4426950408889634
SB_SOFTPLUS_LINEAR = 64.0


def _sb_body(q_ref, k_ref, v_ref, tri_ref, o_ref, acc_ref, c_ref, z_ref, sp_ref, u0_ref, u_ref):
    i = pl.program_id(1)
    qi = i * SB_TQ
    acc_ref[...] = jnp.zeros_like(acc_ref)
    c_ref[...] = jnp.zeros_like(c_ref)
    row = lax.broadcasted_iota(jnp.int32, (SB_TQ, SB_TK), 0)
    col = lax.broadcasted_iota(jnp.int32, (SB_TQ, SB_TK), 1)

    def key_start(t):
        return pl.multiple_of(qi + (SB_ND - 1 - t) * SB_TK, SB_TK)

    def p1(t, slot):
        kb = k_ref[pl.ds(key_start(t), SB_TK), :]
        z_ref[slot] = lax.dot_general(q_ref[...], kb, NT_DIMS, preferred_element_type=F32)

    def p2(t, slot, masked):
        z = z_ref[slot]
        sp = jnp.where(z > SB_SOFTPLUS_LINEAR, z, jnp.log2(1.0 + jnp.exp2(z)))
        c = c_ref[...]
        u0 = z - sp + jnp.concatenate([c, c], axis=1)
        if masked:
            mask = (col + (SB_ND - 1 - t) * SB_TK) < row
            sp = jnp.where(mask, sp, 0.0)
            u0 = jnp.where(mask, u0, NEG)
        sp_ref[slot] = sp.astype(BF16)
        u0_ref[slot] = u0
        c_ref[...] = c - jnp.sum(sp, axis=-1, keepdims=True)

    def p3(slot):
        u = u0_ref[slot] + jnp.dot(sp_ref[slot], tri_ref[...], preferred_element_type=F32)
        u_ref[slot] = u.astype(BF16)

    def p4(t, slot):
        vb = v_ref[pl.ds(key_start(t), SB_TK), :]
        acc_ref[...] += jnp.dot(jnp.exp2(u_ref[slot]), vb, preferred_element_type=F32)

    n = (i + 1) * SB_ND

    def trip(it, slot, masked=False, stages=(True, True, True, True)):
        if stages[3]:
            p4(it - 3, 1 - slot)
        if stages[2]:
            p3(slot)
        if stages[1]:
            p2(it - 1, 1 - slot, masked)
        if stages[0]:
            p1(it, slot)

    @pl.when(i == 0)
    def _():
        for t in range(SB_ND):
            p1(t, t % 2)
            p2(t, t % 2, True)
            p3(t % 2)
            p4(t, t % 2)

    @pl.when(i > 0)
    def _():
        for it in range(SB_ND + 1):
            trip(it, it % 2, True, (True, it >= 1, it >= 2, it >= 3))

        def body(it, carry):
            for slot in range(2):
                @pl.when(it % 2 == slot)
                def _():
                    trip(it, slot)
            return carry

        lax.fori_loop(SB_ND + 1, n, body, 0)
        for d in range(3):
            trip(n + d, d % 2, False, (False, d < 1, d < 2, True))

    o_ref[...] = acc_ref[...].astype(o_ref.dtype)


def _stick_breaking(pm, tri):
    s = pm.shape[0]
    qb, kb, vb = OFF_SQ // 128, OFF_SK // 128, OFF_SV // 128
    return pl.pallas_call(
        _sb_body,
        grid=(SB_HEADS, s // SB_TQ),
        in_specs=[
            pl.BlockSpec((SB_TQ, 128), lambda h, i: (i, qb + h)),
            pl.BlockSpec((s, 128), lambda h, i: (0, kb + h)),
            pl.BlockSpec((s, 128), lambda h, i: (0, vb + h)),
            _const_spec((SB_TK, SB_TK), (0, 0)),
        ],
        out_specs=pl.BlockSpec((SB_TQ, 128), lambda h, i: (i, h)),
        out_shape=jax.ShapeDtypeStruct((s, SB_WIDTH), BF16),
        scratch_shapes=[pltpu.VMEM((SB_TQ, 128), F32), pltpu.VMEM((SB_TQ, 128), F32),
                        pltpu.VMEM((2, SB_TQ, SB_TK), F32), pltpu.VMEM((2, SB_TQ, SB_TK), BF16),
                        pltpu.VMEM((2, SB_TQ, SB_TK), F32), pltpu.VMEM((2, SB_TQ, SB_TK), BF16)],
        compiler_params=_params(("parallel", "parallel")),
        name="stick_breaking",
    )(pm, pm, pm, tri)


def _ret_body(q_ref, k_ref, v01_ref, v23_ref, g01_ref, g23_ref, cos_ref, sin_ref, perm_ref, dec_ref,
              zeta_ref, xi_ref, cd_ref, nw_ref, o_ref, r_ref):
    n = pl.program_id(0)
    v_refs = (v01_ref, v23_ref)
    g_refs = (g01_ref, g23_ref)

    @pl.when(n == 0)
    def _():
        r_ref[...] = jnp.zeros_like(r_ref)

    cos = cos_ref[...]
    sin = sin_ref[...]
    perm = perm_ref[...]

    def rot(x):
        swapped = jnp.dot(x, perm, preferred_element_type=F32)
        return x.astype(F32) * cos + swapped * sin

    rq = rot(q_ref[...])
    rk = rot(k_ref[...])
    qb = rq.astype(BF16)
    qx = (rq * xi_ref[...]).astype(BF16)
    kb = rk.astype(BF16)
    kz = (rk * zeta_ref[...]).astype(BF16)
    lane = lax.broadcasted_iota(jnp.int32, (RET_CHUNK, 128), 1)
    lo = lane < RET_DK
    zero = jnp.zeros((RET_CHUNK, 128), BF16)
    for h in range(RET_HEADS):
        sl = slice((h // 2) * 128, (h // 2 + 1) * 128)
        sel = lo if h % 2 == 0 else jnp.logical_not(lo)
        hs = slice((h % 2) * RET_DV, (h % 2 + 1) * RET_DV)
        v = v_refs[h // 2][:, hs]
        qm = jnp.where(sel, qb[:, sl], zero)
        qxm = jnp.where(sel, qx[:, sl], zero)
        intra = lax.dot_general(qm, kb[:, sl], NT_DIMS, preferred_element_type=F32) * dec_ref[h]
        o = jnp.dot(intra.astype(BF16), v, preferred_element_type=F32)
        o = o + jnp.dot(qxm, r_ref[h].astype(BF16), preferred_element_type=F32)
        kv = lax.dot_general(kz[:, sl], v, TN_DIMS, preferred_element_type=F32)
        r_ref[h] = r_ref[h] * cd_ref[h] + kv
        mu = jnp.mean(o, axis=-1, keepdims=True)
        d = o - mu
        var = jnp.mean(d * d, axis=-1, keepdims=True)
        on = (d * lax.rsqrt(var + EPS)) * nw_ref[:, h * RET_DV:(h + 1) * RET_DV]
        gate = g_refs[h // 2][:, hs].astype(F32)
        o_ref[:, h * RET_DV:(h + 1) * RET_DV] = (gate * jax.nn.sigmoid(gate) * on).astype(o_ref.dtype)


def _retention(pm, tabs, ret_nw3, l):
    s = pm.shape[0]
    cos_t, sin_t, perm, dec, zeta, xi, cd = tabs
    c = RET_CHUNK
    return pl.pallas_call(
        _ret_body,
        grid=(s // c,),
        in_specs=[
            pl.BlockSpec((c, 256), lambda n: (n, OFF_RQ // 256)),
            pl.BlockSpec((c, 256), lambda n: (n, OFF_RK // 256)),
            pl.BlockSpec((c, 256), lambda n: (n, OFF_RV // 256)),
            pl.BlockSpec((c, 256), lambda n: (n, OFF_RV // 256 + 1)),
            pl.BlockSpec((c, 256), lambda n: (n, OFF_RG // 256)),
            pl.BlockSpec((c, 256), lambda n: (n, OFF_RG // 256 + 1)),
            pl.BlockSpec((c, 256), lambda n: (n, 0)),
            pl.BlockSpec((c, 256), lambda n: (n, 0)),
            _const_spec((256, 256), (0, 0)),
            _const_spec((RET_HEADS, c, c), (0, 0, 0)),
            _const_spec((c, 256), (0, 0)),
            _const_spec((c, 256), (0, 0)),
            _const_spec((RET_HEADS, 128, RET_DV), (0, 0, 0)),
            _const_spec((None, 1, RET_V_WIDTH), (l, 0, 0)),
        ],
        out_specs=pl.BlockSpec((c, RET_V_WIDTH), lambda n: (n, 0)),
        out_shape=jax.ShapeDtypeStruct((s, RET_V_WIDTH), BF16),
        scratch_shapes=[pltpu.VMEM((RET_HEADS, 128, RET_DV), F32)],
        compiler_params=_params(("arbitrary",)),
        name="retention",
    )(pm, pm, pm, pm, pm, pm, cos_t, sin_t, perm, dec, zeta, xi, cd, ret_nw3)


MG_TM = 512


def _merge_body(x_ref, o0, o1, o2, l0, l1, l2, yb_ref, yc_ref, gi_ref, bg_ref,
                wa_ref, wb_ref, wc_ref, wo_ref, out_ref, so1, so2, sl1, sl2):
    for src, dst, d in ((o1, so1, DIL_GROUPS[1][1]), (l1, sl1, DIL_GROUPS[1][1]),
                        (o2, so2, DIL_GROUPS[2][1]), (l2, sl2, DIL_GROUPS[2][1])):
        for r in range(d):
            for j in range(A_OUT // 128):
                col = r * A_OUT + j * 128
                dst[j, pl.ds(r, MG_TM // d, stride=d), :] = src[:, col:col + 128]

    def tok(ref):
        return jnp.concatenate([ref[j] for j in range(A_OUT // 128)], axis=1)

    a0, a1, a2 = l0[...], tok(sl1), tok(sl2)
    m = jnp.maximum(jnp.maximum(a0, a1), a2)
    e0, e1, e2 = jnp.exp(a0 - m), jnp.exp(a1 - m), jnp.exp(a2 - m)
    ya = (e0 * o0[...] + e1 * tok(so1) + e2 * tok(so2)) / (e0 + e1 + e2)
    pa = jnp.dot(ya.astype(BF16), wa_ref[...], preferred_element_type=F32)
    pb = jnp.dot(yb_ref[...], wb_ref[...], preferred_element_type=F32)
    pc = jnp.dot(yc_ref[...], wc_ref[...], preferred_element_type=F32)
    merged = None
    for b, pr in enumerate((pa, pb, pc)):
        gate = jax.nn.sigmoid(gi_ref[:, b * D_MODEL:(b + 1) * D_MODEL].astype(F32) + bg_ref[b:b + 1, :])
        merged = gate * pr if merged is None else merged + gate * pr
    out_ref[...] = x_ref[...] + jnp.dot(merged.astype(BF16), wo_ref[...], preferred_element_type=F32)


def _merge(x, oas, lses, yb, yc, gi, b_gate, wa, wb, wc, wo, l):
    s = x.shape[0]
    tm = MG_TM
    row = lambda w: pl.BlockSpec((tm, w), lambda i: (i, 0))
    grp = [pl.BlockSpec((tm // d, d * A_OUT), lambda i: (i, 0)) for _, d in DIL_GROUPS]
    return pl.pallas_call(
        _merge_body,
        grid=(s // tm,),
        scratch_shapes=[pltpu.VMEM((A_OUT // 128, tm, 128), F32)] * 4,
        in_specs=[row(D_MODEL)] + grp + grp + [row(SB_WIDTH), row(RET_V_WIDTH), row(GATE_W),
                  _const_spec((None, N_BRANCH, D_MODEL), (l, 0, 0)),
                  _const_spec((None, A_OUT, D_MODEL), (l, 0, 0)),
                  _const_spec((None, SB_WIDTH, D_MODEL), (l, 0, 0)),
                  _const_spec((None, RET_V_WIDTH, D_MODEL), (l, 0, 0)),
                  _const_spec((None, D_MODEL, D_MODEL), (l, 0, 0))],
        out_specs=row(D_MODEL),
        out_shape=jax.ShapeDtypeStruct((s, D_MODEL), F32),
        compiler_params=_params(("parallel",)),
        name="merge_outproj",
    )(x, *oas, *lses, yb, yc, gi, b_gate, wa, wb, wc, wo)


FF_TM = 512
FF_CH = 256
CARRY = 8


def _gelu_tanh(x):
    return 0.5 * x * (1.0 + jnp.tanh(np.sqrt(2.0 / np.pi).astype(np.float32) * (x + 0.044715 * (x * x * x))))


def _ffn_body(x_ref, nw_ref, wu_ref, wg_ref, cw_ref, cb_ref, wd_ref, out_ref, u_ref, a_ref):
    i = pl.program_id(0)

    @pl.when(i == 0)
    def _():
        u_ref[0:CARRY, :] = jnp.zeros((CARRY, D_FF), F32)

    x = x_ref[...]
    h = (x * lax.rsqrt(jnp.mean(x * x, axis=-1, keepdims=True) + EPS) * nw_ref[...]).astype(BF16)
    for c in range(D_FF // FF_CH):
        sl = slice(c * FF_CH, (c + 1) * FF_CH)
        u_ref[CARRY:CARRY + FF_TM, sl] = jnp.dot(h, wu_ref[:, sl], preferred_element_type=F32)
        u0 = u_ref[CARRY:CARRY + FF_TM, sl]
        u1 = u_ref[CARRY - 1:CARRY - 1 + FF_TM, sl]
        u2 = u_ref[CARRY - 2:CARRY - 2 + FF_TM, sl]
        uc = cw_ref[0:1, sl] * u2 + cw_ref[1:2, sl] * u1 + cw_ref[2:3, sl] * u0 + cb_ref[:, sl]
        g = jnp.dot(h, wg_ref[:, sl], preferred_element_type=F32)
        a_ref[:, sl] = (_gelu_tanh(uc) * g).astype(BF16)
    u_ref[0:CARRY, :] = u_ref[FF_TM:FF_TM + CARRY, :]
    out_ref[...] = x + jnp.dot(a_ref[...], wd_ref[...], preferred_element_type=F32)


def _conv_ffn(x, nw3, wu, wg, cw, cb3, wd, l):
    s = x.shape[0]
    tm = FF_TM
    return pl.pallas_call(
        _ffn_body,
        grid=(s // tm,),
        in_specs=[
            pl.BlockSpec((tm, D_MODEL), lambda i: (i, 0)),
            _const_spec((None, 1, D_MODEL), (l, 0, 0)),
            _const_spec((None, D_MODEL, D_FF), (l, 0, 0)),
            _const_spec((None, D_MODEL, D_FF), (l, 0, 0)),
            _const_spec((None, 3, D_FF), (l, 0, 0)),
            _const_spec((None, 1, D_FF), (l, 0, 0)),
            _const_spec((None, D_FF, D_MODEL), (l, 0, 0)),
        ],
        out_specs=pl.BlockSpec((tm, D_MODEL), lambda i: (i, 0)),
        out_shape=jax.ShapeDtypeStruct((s, D_MODEL), F32),
        scratch_shapes=[pltpu.VMEM((tm + CARRY, D_FF), F32), pltpu.VMEM((tm, D_FF), BF16)],
        compiler_params=_params(("arbitrary",)),
        name="conv_ffn",
    )(x, nw3, wu, wg, cw, cb3, wd)


def _norm_body(x_ref, w_ref, o_ref):
    x = x_ref[...]
    o_ref[...] = x * lax.rsqrt(jnp.mean(x * x, axis=-1, keepdims=True) + EPS) * w_ref[...]


def _final_norm(x, w):
    s = x.shape[0]
    tm = 1024
    return pl.pallas_call(
        _norm_body,
        grid=(s // tm,),
        in_specs=[pl.BlockSpec((tm, D_MODEL), lambda i: (i, 0)), _const_spec((1, D_MODEL), (0, 0))],
        out_specs=pl.BlockSpec((tm, D_MODEL), lambda i: (i, 0)),
        out_shape=jax.ShapeDtypeStruct((s, D_MODEL), F32),
        compiler_params=_params(("parallel",)),
        name="final_norm",
    )(x, w)


def _t5_causal_bucket(dist):
    max_exact = REL_BUCKETS // 2
    n = np.maximum(dist, 0)
    large = max_exact + (np.log(np.maximum(n, 1) / max_exact) / np.log(REL_MAX_DIST / max_exact)
                         * (REL_BUCKETS - max_exact)).astype(np.int64)
    large = np.minimum(large, REL_BUCKETS - 1)
    return np.where(n < max_exact, n, large).astype(np.int32)


def _dilated_bias(rel_bias, g, window, dilation):
    n_steps = window // dilation
    steps = np.arange(BLOCK)[:, None] + BLOCK - np.arange(2 * BLOCK)[None, :]
    in_band = (steps >= 0) & (steps <= n_steps)
    bucket = _t5_causal_bucket(steps * dilation)
    bias_g = rel_bias[:, g * A_HEADS_PER_GROUP:(g + 1) * A_HEADS_PER_GROUP].astype(F32)
    onehot = (jnp.asarray(bucket)[:, :, None] == jnp.arange(REL_BUCKETS)[None, None, :]).astype(F32)
    bias = jnp.einsum('qkb,bh->hqk', onehot, bias_g, precision=lax.Precision.HIGHEST)
    return jnp.where(jnp.asarray(in_band)[None], bias, NEG)


def _retention_tables(s):
    half = RET_DK // 2
    inv = ROPE_BASE ** (-jnp.arange(half, dtype=F32) / half)
    ang = jnp.arange(s, dtype=jnp.int32).astype(F32)[:, None] * inv[None, :]
    cos, sin = jnp.cos(ang), jnp.sin(ang)
    cos_t = jnp.tile(jnp.concatenate([cos, cos], axis=-1), (1, RET_HEADS))
    sin_t = jnp.tile(jnp.concatenate([-sin, sin], axis=-1), (1, RET_HEADS))
    j = np.arange(RET_QK_WIDTH)
    partner = np.where((j % RET_DK) < half, j + half, j - half)
    perm = np.zeros((RET_QK_WIDTH, RET_QK_WIDTH), np.float32)
    perm[partner, j] = 1.0
    log_gamma = jnp.log1p(-jnp.exp2(-5.0 - jnp.arange(RET_HEADS, dtype=F32)))
    n = jnp.arange(RET_CHUNK, dtype=F32)
    diff = n[:, None] - n[None, :]
    dec = jnp.where(diff >= 0, jnp.exp(diff[None] * log_gamma[:, None, None]), 0.0)
    zeta = jnp.exp((RET_CHUNK - 1 - n)[:, None] * log_gamma[None, :])
    xi = jnp.exp((n + 1)[:, None] * log_gamma[None, :])
    zeta = jnp.repeat(zeta, RET_DK, axis=1)
    xi = jnp.repeat(xi, RET_DK, axis=1)
    cd = jnp.broadcast_to(jnp.exp(RET_CHUNK * log_gamma)[:, None, None], (RET_HEADS, 128, RET_DV))
    return cos_t, sin_t, jnp.asarray(perm, BF16), dec, zeta, xi, cd


def _scale_row():
    sc = np.ones((1, MAIN_W), np.float32)
    sc[:, 0:A_WIDTH] = HEAD_DIM ** -0.5
    sc[:, A_W + OFF_SQ:A_W + OFF_SQ + SB_WIDTH] = SB_HEAD_DIM ** -0.5 * LOG2E
    sc[:, A_W + OFF_RK:A_W + OFF_RK + RET_QK_WIDTH] = RET_DK ** -0.5
    return jnp.asarray(sc)


def _tri_table():
    k = np.arange(SB_TK)[:, None]
    j = np.arange(SB_TK)[None, :]
    return jnp.asarray(-(k > j).astype(np.float32), BF16)


def kernel(x, rel_bias, norm_mix_w, w_in, b_gate, ret_norm_w, w_proj_a, w_proj_b, w_proj_c, w_out,
           norm_ffn_w, w_up, w_gate, conv_w, conv_b, w_down, final_norm_w):
    b, s, _ = x.shape
    assert b == 1
    xs = x.reshape(s, D_MODEL)
    w_in_bf = w_in.astype(BF16)
    wa, wb, wc, wo = (w.astype(BF16) for w in (w_proj_a, w_proj_b, w_proj_c, w_out))
    wu, wg, wd = (w.astype(BF16) for w in (w_up, w_gate, w_down))
    nmix3 = norm_mix_w.reshape(DEPTH, 1, D_MODEL)
    nffn3 = norm_ffn_w.reshape(DEPTH, 1, D_MODEL)
    rnw3 = ret_norm_w.reshape(DEPTH, 1, RET_V_WIDTH)
    cb3 = conv_b.reshape(DEPTH, 1, D_FF)
    scale_row = _scale_row()
    tri = _tri_table()
    biases = [_dilated_bias(rel_bias, g, w, d) for g, (w, d) in enumerate(DIL_GROUPS)]
    rtabs = _retention_tables(s)

    for l in range(DEPTH):
        a0, a1, a2, pm, gi = _inproj(xs, nmix3, scale_row, w_in_bf, l)
        oas, lses = [], []
        for g, (a_g, (_, dilation)) in enumerate(zip((a0, a1, a2), DIL_GROUPS)):
            o_g, lse_g = _dilated_group(a_g, biases[g], dilation)
            oas.append(o_g)
            lses.append(lse_g)
        yb = _stick_breaking(pm, tri)
        yc = _retention(pm, rtabs, rnw3, l)
        xs = _merge(xs, oas, lses, yb, yc, gi, b_gate, wa, wb, wc, wo, l)
        xs = _conv_ffn(xs, nffn3, wu, wg, conv_w, cb3, wd, l)
    return _final_norm(xs, final_norm_w.reshape(1, D_MODEL)).reshape(b, s, D_MODEL)
```

```python
import functools

import numpy as np
import jax
import jax.numpy as jnp
from jax import lax
from jax.experimental import pallas as pl
from jax.experimental.pallas import tpu as pltpu

F32 = jnp.float32
BF16 = jnp.bfloat16

D_MODEL = 1024
DEPTH = 4
HEAD_DIM = 64
BLOCK = 128
DIL_GROUPS = ((128, 1), (512, 4), (2048, 16))
N_GROUPS = 3
A_HEADS_PER_GROUP = 4
A_WIDTH = 768
A_OUT = 256
SB_HEADS = 4
SB_HEAD_DIM = 128
SB_WIDTH = 512
RET_HEADS = 4
RET_DK = 64
RET_DV = 128
RET_QK_WIDTH = 256
RET_V_WIDTH = 512
RET_CHUNK = 128
ROPE_BASE = 10000.0
REL_BUCKETS = 32
REL_MAX_DIST = 2048
N_BRANCH = 3
D_FF = 2816
EPS = 1e-6

GATE_W = N_BRANCH * D_MODEL
A_W = 3 * A_WIDTH
PM_W = 3 * SB_WIDTH + 2 * RET_QK_WIDTH + 2 * RET_V_WIDTH
MAIN_W = A_W + PM_W
IN_WIDTH = MAIN_W + GATE_W
G_W = 3 * A_OUT
OFF_SQ, OFF_SK, OFF_SV = 0, 512, 1024
OFF_RQ, OFF_RK, OFF_RV, OFF_RG = 1536, 1792, 2048, 2560

NEG = -1e30
VMEM_LIMIT = 56 * 1024 * 1024

NT_DIMS = (((1,), (1,)), ((), ()))
TN_DIMS = (((0,), (0,)), ((), ()))


def _params(sem, vmem=VMEM_LIMIT):
    return pltpu.CompilerParams(dimension_semantics=sem, vmem_limit_bytes=vmem)


def _const_spec(shape, index):
    nd = len(index)
    return pl.BlockSpec(shape, lambda *_: index, pipeline_mode=pl.Buffered(1))


IN_TM = 512
IN_CH = 768


def _inproj_body(x_ref, nw_ref, sc_ref, w_ref, a0_ref, a1_ref, a2_ref, pm_ref, gi_ref, ys_ref):
    x = x_ref[...]
    h = (x * lax.rsqrt(jnp.mean(x * x, axis=-1, keepdims=True) + EPS) * nw_ref[...]).astype(BF16)
    a_refs = (a0_ref, a1_ref, a2_ref)
    for c in range(A_W // IN_CH):
        sl = slice(c * IN_CH, (c + 1) * IN_CH)
        y = jnp.dot(h, w_ref[:, sl], preferred_element_type=F32) * sc_ref[:, sl]
        for j in range(IN_CH // 128):
            ys_ref[j] = y[:, j * 128:(j + 1) * 128]
        for g, (_, d) in enumerate(DIL_GROUPS):
            for r in range(d):
                rows = pl.ds(r, IN_TM // d, stride=d) if d > 1 else slice(None)
                for j in range(A_OUT // 128):
                    col = r * G_W + c * A_OUT + j * 128
                    a_refs[g][:, col:col + 128] = ys_ref[g * (A_OUT // 128) + j, rows, :].astype(BF16)
    for c in range(PM_W // IN_CH):
        sl = slice(A_W + c * IN_CH, A_W + (c + 1) * IN_CH)
        y = jnp.dot(h, w_ref[:, sl], preferred_element_type=F32)
        pm_ref[:, c * IN_CH:(c + 1) * IN_CH] = (y * sc_ref[:, sl]).astype(BF16)
    for c in range(GATE_W // IN_CH):
        y = jnp.dot(h, w_ref[:, MAIN_W + c * IN_CH:MAIN_W + (c + 1) * IN_CH], preferred_element_type=F32)
        gi_ref[:, c * IN_CH:(c + 1) * IN_CH] = y.astype(BF16)


def _inproj(x, norm_w3, scale_row, w_in_bf, l):
    s = x.shape[0]
    a_shapes = [jax.ShapeDtypeStruct((s // d, d * G_W), BF16) for _, d in DIL_GROUPS]
    a_specs = [pl.BlockSpec((IN_TM // d, d * G_W), lambda i: (i, 0)) for _, d in DIL_GROUPS]
    return pl.pallas_call(
        _inproj_body,
        grid=(s // IN_TM,),
        in_specs=[
            pl.BlockSpec((IN_TM, D_MODEL), lambda i: (i, 0)),
            _const_spec((None, 1, D_MODEL), (l, 0, 0)),
            _const_spec((1, MAIN_W), (0, 0)),
            _const_spec((None, D_MODEL, IN_WIDTH), (l, 0, 0)),
        ],
        out_specs=a_specs + [
            pl.BlockSpec((IN_TM, PM_W), lambda i: (i, 0)),
            pl.BlockSpec((IN_TM, GATE_W), lambda i: (i, 0)),
        ],
        out_shape=a_shapes + [jax.ShapeDtypeStruct((s, PM_W), BF16), jax.ShapeDtypeStruct((s, GATE_W), BF16)],
        scratch_shapes=[pltpu.VMEM((IN_CH // 128, IN_TM, 128), F32)],
        compiler_params=_params(("parallel",)),
        name="inproj",
    )(x, norm_w3, scale_row, w_in_bf)


DIL_NB = 4


def _dil_body(q_ref, k_ref, ke_ref, v_ref, ve_ref, bias_ref, o_ref, lse_ref, *, nblk):
    n = pl.program_id(1)
    lane = lax.broadcasted_iota(jnp.int32, (BLOCK, 128), 1)
    lo = lane < HEAD_DIM
    col = lax.broadcasted_iota(jnp.int32, (BLOCK, 2 * BLOCK), 1)
    first_ok = jnp.logical_or(n > 0, col >= BLOCK)
    for b in range(nblk):
        for p in range(2):
            sl = slice(p * 128, (p + 1) * 128)
            q = q_ref[b * BLOCK:(b + 1) * BLOCK, sl]
            if b == 0:
                k2 = jnp.concatenate([ke_ref[:, sl], k_ref[0:BLOCK, sl]], axis=0)
                v2 = jnp.concatenate([ve_ref[:, sl], v_ref[0:BLOCK, sl]], axis=0)
            else:
                k2 = k_ref[(b - 1) * BLOCK:(b + 1) * BLOCK, sl]
                v2 = v_ref[(b - 1) * BLOCK:(b + 1) * BLOCK, sl]
            outs, lses = [], []
            for hh in range(2):
                h = 2 * p + hh
                sel = lo if hh == 0 else jnp.logical_not(lo)
                qm = jnp.where(sel, q, jnp.zeros_like(q))
                s = lax.dot_general(qm, k2, NT_DIMS, preferred_element_type=F32) + bias_ref[h]
                if b == 0:
                    s = jnp.where(first_ok, s, NEG)
                m = jnp.max(s, axis=-1, keepdims=True)
                e = jnp.exp(s - m)
                den = jnp.sum(e, axis=-1, keepdims=True)
                num = jnp.dot(e.astype(BF16), v2, preferred_element_type=F32)
                outs.append(num / den)
                lses.append(jnp.broadcast_to(m + jnp.log(den), (BLOCK, 128)))
            o_ref[b * BLOCK:(b + 1) * BLOCK, sl] = jnp.where(lo, outs[0], outs[1])
            lse_ref[b * BLOCK:(b + 1) * BLOCK, sl] = jnp.where(lo, lses[0], lses[1])


def _dilated_group(a_g, bias_g, dilation):
    rows = a_g.shape[0]
    nblk = min(DIL_NB, rows // BLOCK)
    tr = nblk * BLOCK
    cb = G_W // A_OUT

    def cur(off):
        return pl.BlockSpec((tr, A_OUT), lambda r, n: (n, r * cb + off))

    def edge(off):
        return pl.BlockSpec((BLOCK, A_OUT), lambda r, n: (jnp.maximum(n * nblk - 1, 0), r * cb + off))

    return pl.pallas_call(
        functools.partial(_dil_body, nblk=nblk),
        grid=(dilation, rows // tr),
        in_specs=[cur(0), cur(1), edge(1), cur(2), edge(2),
                  _const_spec((A_HEADS_PER_GROUP, BLOCK, 2 * BLOCK), (0, 0, 0))],
        out_specs=[pl.BlockSpec((tr, A_OUT), lambda r, n: (n, r)),
                   pl.BlockSpec((tr, A_OUT), lambda r, n: (n, r))],
        out_shape=[jax.ShapeDtypeStruct((rows, dilation * A_OUT), F32),
                   jax.ShapeDtypeStruct((rows, dilation * A_OUT), F32)],
        compiler_params=_params(("parallel", "parallel")),
        name=f"dilated_attn_d{dilation}",
    )(a_g, a_g, a_g, a_g, a_g, bias_g)


SB_TQ = 1024
SB_TK = 256
SB_ND = SB_TQ // SB_TK
LOG2E = 1.4426950408889634
SB_SOFTPLUS_LINEAR = 64.0
SB_UNDERFLOW_LOG2 = -160.0


def _sb_body(q_ref, k_ref, v_ref, tri_ref, o_ref, acc_ref, c_ref, z_ref, w_ref, live_ref):
    i = pl.program_id(1)
    qi = i * SB_TQ
    acc_ref[...] = jnp.zeros_like(acc_ref)
    c_ref[...] = jnp.zeros_like(c_ref)
    row = lax.broadcasted_iota(jnp.int32, (SB_TQ, SB_TK), 0)
    col = lax.broadcasted_iota(jnp.int32, (SB_TQ, SB_TK), 1)

    def key_start(t):
        return pl.multiple_of(qi + (SB_ND - 1 - t) * SB_TK, SB_TK)

    def p1(t, slot):
        kb = k_ref[pl.ds(key_start(t), SB_TK), :]
        z_ref[slot] = lax.dot_general(q_ref[...], kb, NT_DIMS, preferred_element_type=F32)

    def p23(t, slot, masked):
        z = z_ref[slot]
        sp = jnp.where(z > SB_SOFTPLUS_LINEAR, z, jnp.log2(1.0 + jnp.exp2(z)))
        c = c_ref[...]
        live_ref[0] = (jnp.max(c) > SB_UNDERFLOW_LOG2).astype(jnp.int32)
        u0 = z - sp + jnp.concatenate([c, c], axis=1)
        if masked:
            mask = (col + (SB_ND - 1 - t) * SB_TK) < row
            sp = jnp.where(mask, sp, 0.0)
            u0 = jnp.where(mask, u0, NEG)
        c_ref[...] = c - jnp.sum(sp, axis=-1, keepdims=True)
        u = u0 + jnp.dot(sp.astype(BF16), tri_ref[...], preferred_element_type=F32)
        w_ref[slot] = jnp.exp2(u.astype(BF16))

    def p4(t, slot):
        vb = v_ref[pl.ds(key_start(t), SB_TK), :]
        acc_ref[...] += jnp.dot(w_ref[slot], vb, preferred_element_type=F32)

    n = (i + 1) * SB_ND

    def trip(it, slot, masked=False, stages=(True, True, True)):
        if stages[2]:
            p4(it - 2, slot)
        if stages[1]:
            p23(it - 1, 1 - slot, masked)
        if stages[0]:
            p1(it, slot)

    @pl.when(i == 0)
    def _():
        for t in range(SB_ND):
            p1(t, t % 2)
            p23(t, t % 2, True)
            p4(t, t % 2)

    @pl.when(i > 0)
    def _():
        for it in range(SB_ND + 1):
            trip(it, it % 2, True, (True, it >= 1, it >= 2))

        def cond(carry):
            it, live = carry
            return jnp.logical_and(it < n, live > 0)

        def body(carry):
            it, _ = carry
            for slot in range(2):
                @pl.when(it % 2 == slot)
                def _():
                    trip(it, slot)
            return it + 1, live_ref[0]

        it_end, _ = lax.while_loop(cond, body, (jnp.int32(SB_ND + 1), jnp.int32(1)))
        for slot in range(2):
            @pl.when(it_end % 2 == slot)
            def _():
                trip(it_end, slot, False, (False, True, True))
                trip(it_end + 1, 1 - slot, False, (False, False, True))

    o_ref[...] = acc_ref[...].astype(o_ref.dtype)


def _stick_breaking(pm, tri):
    s = pm.shape[0]
    qb, kb, vb = OFF_SQ // 128, OFF_SK // 128, OFF_SV // 128
    return pl.pallas_call(
        _sb_body,
        grid=(SB_HEADS, s // SB_TQ),
        in_specs=[
            pl.BlockSpec((SB_TQ, 128), lambda h, i: (i, qb + h)),
            pl.BlockSpec((s, 128), lambda h, i: (0, kb + h)),
            pl.BlockSpec((s, 128), lambda h, i: (0, vb + h)),
            _const_spec((SB_TK, SB_TK), (0, 0)),
        ],
        out_specs=pl.BlockSpec((SB_TQ, 128), lambda h, i: (i, h)),
        out_shape=jax.ShapeDtypeStruct((s, SB_WIDTH), BF16),
        scratch_shapes=[pltpu.VMEM((SB_TQ, 128), F32), pltpu.VMEM((SB_TQ, 128), F32),
                        pltpu.VMEM((2, SB_TQ, SB_TK), F32), pltpu.VMEM((2, SB_TQ, SB_TK), BF16),
                        pltpu.SMEM((1,), jnp.int32)],
        compiler_params=_params(("parallel", "parallel")),
        name="stick_breaking",
    )(pm, pm, pm, tri)


def _ret_body(q_ref, k_ref, v01_ref, v23_ref, g01_ref, g23_ref, cos_ref, sin_ref, perm_ref, dec_ref,
              zeta_ref, xi_ref, cd_ref, nw_ref, o_ref, r_ref):
    n = pl.program_id(0)
    v_refs = (v01_ref, v23_ref)
    g_refs = (g01_ref, g23_ref)

    @pl.when(n == 0)
    def _():
        r_ref[...] = jnp.zeros_like(r_ref)

    cos = cos_ref[...]
    sin = sin_ref[...]
    perm = perm_ref[...]

    def rot(x):
        swapped = jnp.dot(x, perm, preferred_element_type=F32)
        return x.astype(F32) * cos + swapped * sin

    rq = rot(q_ref[...])
    rk = rot(k_ref[...])
    qb = rq.astype(BF16)
    qx = (rq * xi_ref[...]).astype(BF16)
    kb = rk.astype(BF16)
    kz = (rk * zeta_ref[...]).astype(BF16)
    lane = lax.broadcasted_iota(jnp.int32, (RET_CHUNK, 128), 1)
    lo = lane < RET_DK
    zero = jnp.zeros((RET_CHUNK, 128), BF16)
    for h in range(RET_HEADS):
        sl = slice((h // 2) * 128, (h // 2 + 1) * 128)
        sel = lo if h % 2 == 0 else jnp.logical_not(lo)
        hs = slice((h % 2) * RET_DV, (h % 2 + 1) * RET_DV)
        v = v_refs[h // 2][:, hs]
        qm = jnp.where(sel, qb[:, sl], zero)
        qxm = jnp.where(sel, qx[:, sl], zero)
        intra = lax.dot_general(qm, kb[:, sl], NT_DIMS, preferred_element_type=F32) * dec_ref[h]
        o = jnp.dot(intra.astype(BF16), v, preferred_element_type=F32)
        o = o + jnp.dot(qxm, r_ref[h].astype(BF16), preferred_element_type=F32)
        kv = lax.dot_general(kz[:, sl], v, TN_DIMS, preferred_element_type=F32)
        r_ref[h] = r_ref[h] * cd_ref[h] + kv
        mu = jnp.mean(o, axis=-1, keepdims=True)
        d = o - mu
        var = jnp.mean(d * d, axis=-1, keepdims=True)
        on = (d * lax.rsqrt(var + EPS)) * nw_ref[:, h * RET_DV:(h + 1) * RET_DV]
        gate = g_refs[h // 2][:, hs].astype(F32)
        o_ref[:, h * RET_DV:(h + 1) * RET_DV] = (gate * jax.nn.sigmoid(gate) * on).astype(o_ref.dtype)


def _retention(pm, tabs, ret_nw3, l):
    s = pm.shape[0]
    cos_t, sin_t, perm, dec, zeta, xi, cd = tabs
    c = RET_CHUNK
    return pl.pallas_call(
        _ret_body,
        grid=(s // c,),
        in_specs=[
            pl.BlockSpec((c, 256), lambda n: (n, OFF_RQ // 256)),
            pl.BlockSpec((c, 256), lambda n: (n, OFF_RK // 256)),
            pl.BlockSpec((c, 256), lambda n: (n, OFF_RV // 256)),
            pl.BlockSpec((c, 256), lambda n: (n, OFF_RV // 256 + 1)),
            pl.BlockSpec((c, 256), lambda n: (n, OFF_RG // 256)),
            pl.BlockSpec((c, 256), lambda n: (n, OFF_RG // 256 + 1)),
            pl.BlockSpec((c, 256), lambda n: (n, 0)),
            pl.BlockSpec((c, 256), lambda n: (n, 0)),
            _const_spec((256, 256), (0, 0)),
            _const_spec((RET_HEADS, c, c), (0, 0, 0)),
            _const_spec((c, 256), (0, 0)),
            _const_spec((c, 256), (0, 0)),
            _const_spec((RET_HEADS, 128, RET_DV), (0, 0, 0)),
            _const_spec((None, 1, RET_V_WIDTH), (l, 0, 0)),
        ],
        out_specs=pl.BlockSpec((c, RET_V_WIDTH), lambda n: (n, 0)),
        out_shape=jax.ShapeDtypeStruct((s, RET_V_WIDTH), BF16),
        scratch_shapes=[pltpu.VMEM((RET_HEADS, 128, RET_DV), F32)],
        compiler_params=_params(("arbitrary",)),
        name="retention",
    )(pm, pm, pm, pm, pm, pm, cos_t, sin_t, perm, dec, zeta, xi, cd, ret_nw3)


MG_TM = 512


def _merge_body(x_ref, o0, o1, o2, l0, l1, l2, yb_ref, yc_ref, gi_ref, bg_ref,
                wa_ref, wb_ref, wc_ref, wo_ref, out_ref, so1, so2, sl1, sl2):
    for src, dst, d in ((o1, so1, DIL_GROUPS[1][1]), (l1, sl1, DIL_GROUPS[1][1]),
                        (o2, so2, DIL_GROUPS[2][1]), (l2, sl2, DIL_GROUPS[2][1])):
        for r in range(d):
            for j in range(A_OUT // 128):
                col = r * A_OUT + j * 128
                dst[j, pl.ds(r, MG_TM // d, stride=d), :] = src[:, col:col + 128]

    def tok(ref):
        return jnp.concatenate([ref[j] for j in range(A_OUT // 128)], axis=1)

    a0, a1, a2 = l0[...], tok(sl1), tok(sl2)
    m = jnp.maximum(jnp.maximum(a0, a1), a2)
    e0, e1, e2 = jnp.exp(a0 - m), jnp.exp(a1 - m), jnp.exp(a2 - m)
    ya = (e0 * o0[...] + e1 * tok(so1) + e2 * tok(so2)) / (e0 + e1 + e2)
    pa = jnp.dot(ya.astype(BF16), wa_ref[...], preferred_element_type=F32)
    pb = jnp.dot(yb_ref[...], wb_ref[...], preferred_element_type=F32)
    pc = jnp.dot(yc_ref[...], wc_ref[...], preferred_element_type=F32)
    merged = None
    for b, pr in enumerate((pa, pb, pc)):
        gate = jax.nn.sigmoid(gi_ref[:, b * D_MODEL:(b + 1) * D_MODEL].astype(F32) + bg_ref[b:b + 1, :])
        merged = gate * pr if merged is None else merged + gate * pr
    out_ref[...] = x_ref[...] + jnp.dot(merged.astype(BF16), wo_ref[...], preferred_element_type=F32)


def _merge(x, oas, lses, yb, yc, gi, b_gate, wa, wb, wc, wo, l):
    s = x.shape[0]
    tm = MG_TM
    row = lambda w: pl.BlockSpec((tm, w), lambda i: (i, 0))
    grp = [pl.BlockSpec((tm // d, d * A_OUT), lambda i: (i, 0)) for _, d in DIL_GROUPS]
    return pl.pallas_call(
        _merge_body,
        grid=(s // tm,),
        scratch_shapes=[pltpu.VMEM((A_OUT // 128, tm, 128), F32)] * 4,
        in_specs=[row(D_MODEL)] + grp + grp + [row(SB_WIDTH), row(RET_V_WIDTH), row(GATE_W),
                  _const_spec((None, N_BRANCH, D_MODEL), (l, 0, 0)),
                  _const_spec((None, A_OUT, D_MODEL), (l, 0, 0)),
                  _const_spec((None, SB_WIDTH, D_MODEL), (l, 0, 0)),
                  _const_spec((None, RET_V_WIDTH, D_MODEL), (l, 0, 0)),
                  _const_spec((None, D_MODEL, D_MODEL), (l, 0, 0))],
        out_specs=row(D_MODEL),
        out_shape=jax.ShapeDtypeStruct((s, D_MODEL), F32),
        compiler_params=_params(("parallel",)),
        name="merge_outproj",
    )(x, *oas, *lses, yb, yc, gi, b_gate, wa, wb, wc, wo)


FF_TM = 512
FF_CH = 256
CARRY = 8


def _gelu_tanh(x):
    return 0.5 * x * (1.0 + jnp.tanh(np.sqrt(2.0 / np.pi).astype(np.float32) * (x + 0.044715 * (x * x * x))))


def _rms(x, w):
    return x * lax.rsqrt(jnp.mean(x * x, axis=-1, keepdims=True) + EPS) * w


def _ffn_body(x_ref, nw_ref, wu_ref, wg_ref, cw_ref, cb_ref, wd_ref, fw_ref, out_ref, u_ref, a_ref, *, final):
    i = pl.program_id(0)

    @pl.when(i == 0)
    def _():
        u_ref[0:CARRY, :] = jnp.zeros((CARRY, D_FF), F32)

    x = x_ref[...]
    h = (x * lax.rsqrt(jnp.mean(x * x, axis=-1, keepdims=True) + EPS) * nw_ref[...]).astype(BF16)
    for c in range(D_FF // FF_CH):
        sl = slice(c * FF_CH, (c + 1) * FF_CH)
        u_ref[CARRY:CARRY + FF_TM, sl] = jnp.dot(h, wu_ref[:, sl], preferred_element_type=F32)
        u0 = u_ref[CARRY:CARRY + FF_TM, sl]
        u1 = u_ref[CARRY - 1:CARRY - 1 + FF_TM, sl]
        u2 = u_ref[CARRY - 2:CARRY - 2 + FF_TM, sl]
        uc = cw_ref[0:1, sl] * u2 + cw_ref[1:2, sl] * u1 + cw_ref[2:3, sl] * u0 + cb_ref[:, sl]
        g = jnp.dot(h, wg_ref[:, sl], preferred_element_type=F32)
        a_ref[:, sl] = (_gelu_tanh(uc) * g).astype(BF16)
    u_ref[0:CARRY, :] = u_ref[FF_TM:FF_TM + CARRY, :]
    y = x + jnp.dot(a_ref[...], wd_ref[...], preferred_element_type=F32)
    out_ref[...] = _rms(y, fw_ref[...]) if final else y


def _conv_ffn(x, nw3, wu, wg, cw, cb3, wd, fw, l):
    s = x.shape[0]
    tm = FF_TM
    return pl.pallas_call(
        functools.partial(_ffn_body, final=(l == DEPTH - 1)),
        grid=(s // tm,),
        in_specs=[
            pl.BlockSpec((tm, D_MODEL), lambda i: (i, 0)),
            _const_spec((None, 1, D_MODEL), (l, 0, 0)),
            _const_spec((None, D_MODEL, D_FF), (l, 0, 0)),
            _const_spec((None, D_MODEL, D_FF), (l, 0, 0)),
            _const_spec((None, 3, D_FF), (l, 0, 0)),
            _const_spec((None, 1, D_FF), (l, 0, 0)),
            _const_spec((None, D_FF, D_MODEL), (l, 0, 0)),
            _const_spec((1, D_MODEL), (0, 0)),
        ],
        out_specs=pl.BlockSpec((tm, D_MODEL), lambda i: (i, 0)),
        out_shape=jax.ShapeDtypeStruct((s, D_MODEL), F32),
        scratch_shapes=[pltpu.VMEM((tm + CARRY, D_FF), F32), pltpu.VMEM((tm, D_FF), BF16)],
        compiler_params=_params(("arbitrary",)),
        name="conv_ffn",
    )(x, nw3, wu, wg, cw, cb3, wd, fw)


def _t5_causal_bucket(dist):
    max_exact = REL_BUCKETS // 2
    n = np.maximum(dist, 0)
    large = max_exact + (np.log(np.maximum(n, 1) / max_exact) / np.log(REL_MAX_DIST / max_exact)
                         * (REL_BUCKETS - max_exact)).astype(np.int64)
    large = np.minimum(large, REL_BUCKETS - 1)
    return np.where(n < max_exact, n, large).astype(np.int32)


def _dilated_bias(rel_bias, g, window, dilation):
    n_steps = window // dilation
    steps = np.arange(BLOCK)[:, None] + BLOCK - np.arange(2 * BLOCK)[None, :]
    in_band = (steps >= 0) & (steps <= n_steps)
    bucket = _t5_causal_bucket(steps * dilation)
    bias_g = rel_bias[:, g * A_HEADS_PER_GROUP:(g + 1) * A_HEADS_PER_GROUP].astype(F32)
    onehot = (jnp.asarray(bucket)[:, :, None] == jnp.arange(REL_BUCKETS)[None, None, :]).astype(F32)
    bias = jnp.einsum('qkb,bh->hqk', onehot, bias_g, precision=lax.Precision.HIGHEST)
    return jnp.where(jnp.asarray(in_band)[None], bias, NEG)


def _retention_tables(s):
    half = RET_DK // 2
    inv = ROPE_BASE ** (-jnp.arange(half, dtype=F32) / half)
    ang = jnp.arange(s, dtype=jnp.int32).astype(F32)[:, None] * inv[None, :]
    cos, sin = jnp.cos(ang), jnp.sin(ang)
    cos_t = jnp.tile(jnp.concatenate([cos, cos], axis=-1), (1, RET_HEADS))
    sin_t = jnp.tile(jnp.concatenate([-sin, sin], axis=-1), (1, RET_HEADS))
    j = np.arange(RET_QK_WIDTH)
    partner = np.where((j % RET_DK) < half, j + half, j - half)
    perm = np.zeros((RET_QK_WIDTH, RET_QK_WIDTH), np.float32)
    perm[partner, j] = 1.0
    log_gamma = jnp.log1p(-jnp.exp2(-5.0 - jnp.arange(RET_HEADS, dtype=F32)))
    n = jnp.arange(RET_CHUNK, dtype=F32)
    diff = n[:, None] - n[None, :]
    dec = jnp.where(diff >= 0, jnp.exp(diff[None] * log_gamma[:, None, None]), 0.0)
    zeta = jnp.exp((RET_CHUNK - 1 - n)[:, None] * log_gamma[None, :])
    xi = jnp.exp((n + 1)[:, None] * log_gamma[None, :])
    zeta = jnp.repeat(zeta, RET_DK, axis=1)
    xi = jnp.repeat(xi, RET_DK, axis=1)
    cd = jnp.broadcast_to(jnp.exp(RET_CHUNK * log_gamma)[:, None, None], (RET_HEADS, 128, RET_DV))
    return cos_t, sin_t, jnp.asarray(perm, BF16), dec, zeta, xi, cd


def _scale_row():
    sc = np.ones((1, MAIN_W), np.float32)
    sc[:, 0:A_WIDTH] = HEAD_DIM ** -0.5
    sc[:, A_W + OFF_SQ:A_W + OFF_SQ + SB_WIDTH] = SB_HEAD_DIM ** -0.5 * LOG2E
    sc[:, A_W + OFF_RK:A_W + OFF_RK + RET_QK_WIDTH] = RET_DK ** -0.5
    return jnp.asarray(sc)


def _tri_table():
    k = np.arange(SB_TK)[:, None]
    j = np.arange(SB_TK)[None, :]
    return jnp.asarray(-(k > j).astype(np.float32), BF16)


def kernel(x, rel_bias, norm_mix_w, w_in, b_gate, ret_norm_w, w_proj_a, w_proj_b, w_proj_c, w_out,
           norm_ffn_w, w_up, w_gate, conv_w, conv_b, w_down, final_norm_w):
    b, s, _ = x.shape
    assert b == 1
    xs = x.reshape(s, D_MODEL)
    w_in_bf = w_in.astype(BF16)
    wa, wb, wc, wo = (w.astype(BF16) for w in (w_proj_a, w_proj_b, w_proj_c, w_out))
    wu, wg, wd = (w.astype(BF16) for w in (w_up, w_gate, w_down))
    nmix3 = norm_mix_w.reshape(DEPTH, 1, D_MODEL)
    nffn3 = norm_ffn_w.reshape(DEPTH, 1, D_MODEL)
    rnw3 = ret_norm_w.reshape(DEPTH, 1, RET_V_WIDTH)
    cb3 = conv_b.reshape(DEPTH, 1, D_FF)
    scale_row = _scale_row()
    tri = _tri_table()
    biases = [_dilated_bias(rel_bias, g, w, d) for g, (w, d) in enumerate(DIL_GROUPS)]
    rtabs = _retention_tables(s)

    for l in range(DEPTH):
        a0, a1, a2, pm, gi = _inproj(xs, nmix3, scale_row, w_in_bf, l)
        oas, lses = [], []
        for g, (a_g, (_, dilation)) in enumerate(zip((a0, a1, a2), DIL_GROUPS)):
            o_g, lse_g = _dilated_group(a_g, biases[g], dilation)
            oas.append(o_g)
            lses.append(lse_g)
        yb = _stick_breaking(pm, tri)
        yc = _retention(pm, rtabs, rnw3, l)
        xs = _merge(xs, oas, lses, yb, yc, gi, b_gate, wa, wb, wc, wo, l)
        xs = _conv_ffn(xs, nffn3, wu, wg, conv_w, cb3, wd, final_norm_w.reshape(1, D_MODEL), l)
    return xs.reshape(b, s, D_MODEL)
```

```python
import functools

import numpy as np
import jax
import jax.numpy as jnp
from jax import lax
from jax.experimental import pallas as pl
from jax.experimental.pallas import tpu as pltpu

F32 = jnp.float32
BF16 = jnp.bfloat16

D_MODEL = 1024
DEPTH = 4
HEAD_DIM = 64
BLOCK = 128
DIL_GROUPS = ((128, 1), (512, 4), (2048, 16))
N_GROUPS = 3
A_HEADS_PER_GROUP = 4
A_WIDTH = 768
A_OUT = 256
SB_HEADS = 4
SB_HEAD_DIM = 128
SB_WIDTH = 512
RET_HEADS = 4
RET_DK = 64
RET_DV = 128
RET_QK_WIDTH = 256
RET_V_WIDTH = 512
RET_CHUNK = 128
ROPE_BASE = 10000.0
REL_BUCKETS = 32
REL_MAX_DIST = 2048
N_BRANCH = 3
D_FF = 2816
EPS = 1e-6

GATE_W = N_BRANCH * D_MODEL
A_W = 3 * A_WIDTH
PM_W = 3 * SB_WIDTH + 2 * RET_QK_WIDTH + 2 * RET_V_WIDTH
MAIN_W = A_W + PM_W
IN_WIDTH = MAIN_W + GATE_W
G_W = 3 * A_OUT
OFF_SQ, OFF_SK, OFF_SV = 0, 512, 1024
OFF_RQ, OFF_RK, OFF_RV, OFF_RG = 1536, 1792, 2048, 2560

NEG = -1e30
VMEM_LIMIT = 56 * 1024 * 1024

NT_DIMS = (((1,), (1,)), ((), ()))
TN_DIMS = (((0,), (0,)), ((), ()))


def _params(sem, vmem=VMEM_LIMIT):
    return pltpu.CompilerParams(dimension_semantics=sem, vmem_limit_bytes=vmem)


def _const_spec(shape, index):
    nd = len(index)
    return pl.BlockSpec(shape, lambda *_: index, pipeline_mode=pl.Buffered(1))


IN_TM = 512
IN_CH = 768


def _inproj_body(x_ref, nw_ref, sc_ref, w_ref, a0_ref, a1_ref, a2_ref, pm_ref, gi_ref, ys_ref):
    x = x_ref[...]
    h = (x * lax.rsqrt(jnp.mean(x * x, axis=-1, keepdims=True) + EPS) * nw_ref[...]).astype(BF16)
    a_refs = (a0_ref, a1_ref, a2_ref)
    for c in range(A_W // IN_CH):
        sl = slice(c * IN_CH, (c + 1) * IN_CH)
        y = jnp.dot(h, w_ref[:, sl], preferred_element_type=F32) * sc_ref[:, sl]
        for j in range(IN_CH // 128):
            ys_ref[j] = y[:, j * 128:(j + 1) * 128]
        for g, (_, d) in enumerate(DIL_GROUPS):
            for r in range(d):
                rows = pl.ds(r, IN_TM // d, stride=d) if d > 1 else slice(None)
                for j in range(A_OUT // 128):
                    col = r * G_W + c * A_OUT + j * 128
                    a_refs[g][:, col:col + 128] = ys_ref[g * (A_OUT // 128) + j, rows, :].astype(BF16)
    for c in range(PM_W // IN_CH):
        sl = slice(A_W + c * IN_CH, A_W + (c + 1) * IN_CH)
        y = jnp.dot(h, w_ref[:, sl], preferred_element_type=F32)
        pm_ref[:, c * IN_CH:(c + 1) * IN_CH] = (y * sc_ref[:, sl]).astype(BF16)
    for c in range(GATE_W // IN_CH):
        y = jnp.dot(h, w_ref[:, MAIN_W + c * IN_CH:MAIN_W + (c + 1) * IN_CH], preferred_element_type=F32)
        gi_ref[:, c * IN_CH:(c + 1) * IN_CH] = y.astype(BF16)


def _inproj(x, norm_w3, scale_row, w_in_bf, l):
    s = x.shape[0]
    a_shapes = [jax.ShapeDtypeStruct((s // d, d * G_W), BF16) for _, d in DIL_GROUPS]
    a_specs = [pl.BlockSpec((IN_TM // d, d * G_W), lambda i: (i, 0)) for _, d in DIL_GROUPS]
    return pl.pallas_call(
        _inproj_body,
        grid=(s // IN_TM,),
        in_specs=[
            pl.BlockSpec((IN_TM, D_MODEL), lambda i: (i, 0)),
            _const_spec((None, 1, D_MODEL), (l, 0, 0)),
            _const_spec((1, MAIN_W), (0, 0)),
            _const_spec((None, D_MODEL, IN_WIDTH), (l, 0, 0)),
        ],
        out_specs=a_specs + [
            pl.BlockSpec((IN_TM, PM_W), lambda i: (i, 0)),
            pl.BlockSpec((IN_TM, GATE_W), lambda i: (i, 0)),
        ],
        out_shape=a_shapes + [jax.ShapeDtypeStruct((s, PM_W), BF16), jax.ShapeDtypeStruct((s, GATE_W), BF16)],
        scratch_shapes=[pltpu.VMEM((IN_CH // 128, IN_TM, 128), F32)],
        compiler_params=_params(("parallel",)),
        name="inproj",
    )(x, norm_w3, scale_row, w_in_bf)


DIL_NB = 4


def _dil_body(q_ref, k_ref, ke_ref, v_ref, ve_ref, bias_ref, o_ref, lse_ref, *, nblk):
    n = pl.program_id(1)
    lane = lax.broadcasted_iota(jnp.int32, (BLOCK, 128), 1)
    lo = lane < HEAD_DIM
    col = lax.broadcasted_iota(jnp.int32, (BLOCK, 2 * BLOCK), 1)
    first_ok = jnp.logical_or(n > 0, col >= BLOCK)
    for b in range(nblk):
        for p in range(2):
            sl = slice(p * 128, (p + 1) * 128)
            q = q_ref[b * BLOCK:(b + 1) * BLOCK, sl]
            if b == 0:
                k2 = jnp.concatenate([ke_ref[:, sl], k_ref[0:BLOCK, sl]], axis=0)
                v2 = jnp.concatenate([ve_ref[:, sl], v_ref[0:BLOCK, sl]], axis=0)
            else:
                k2 = k_ref[(b - 1) * BLOCK:(b + 1) * BLOCK, sl]
                v2 = v_ref[(b - 1) * BLOCK:(b + 1) * BLOCK, sl]
            outs, lses = [], []
            for hh in range(2):
                h = 2 * p + hh
                sel = lo if hh == 0 else jnp.logical_not(lo)
                qm = jnp.where(sel, q, jnp.zeros_like(q))
                s = lax.dot_general(qm, k2, NT_DIMS, preferred_element_type=F32) + bias_ref[h]
                if b == 0:
                    s = jnp.where(first_ok, s, NEG)
                m = jnp.max(s, axis=-1, keepdims=True)
                e = jnp.exp(s - m)
                den = jnp.sum(e, axis=-1, keepdims=True)
                num = jnp.dot(e.astype(BF16), v2, preferred_element_type=F32)
                outs.append(num / den)
                lses.append(jnp.broadcast_to(m + jnp.log(den), (BLOCK, 128)))
            o_ref[b * BLOCK:(b + 1) * BLOCK, sl] = jnp.where(lo, outs[0], outs[1])
            lse_ref[b * BLOCK:(b + 1) * BLOCK, sl] = jnp.where(lo, lses[0], lses[1])


def _dilated_group(a_g, bias_g, dilation):
    rows = a_g.shape[0]
    nblk = min(DIL_NB, rows // BLOCK)
    tr = nblk * BLOCK
    cb = G_W // A_OUT

    def cur(off):
        return pl.BlockSpec((tr, A_OUT), lambda r, n: (n, r * cb + off))

    def edge(off):
        return pl.BlockSpec((BLOCK, A_OUT), lambda r, n: (jnp.maximum(n * nblk - 1, 0), r * cb + off))

    return pl.pallas_call(
        functools.partial(_dil_body, nblk=nblk),
        grid=(dilation, rows // tr),
        in_specs=[cur(0), cur(1), edge(1), cur(2), edge(2),
                  _const_spec((A_HEADS_PER_GROUP, BLOCK, 2 * BLOCK), (0, 0, 0))],
        out_specs=[pl.BlockSpec((tr, A_OUT), lambda r, n: (n, r)),
                   pl.BlockSpec((tr, A_OUT), lambda r, n: (n, r))],
        out_shape=[jax.ShapeDtypeStruct((rows, dilation * A_OUT), F32),
                   jax.ShapeDtypeStruct((rows, dilation * A_OUT), F32)],
        compiler_params=_params(("parallel", "parallel")),
        name=f"dilated_attn_d{dilation}",
    )(a_g, a_g, a_g, a_g, a_g, bias_g)


SB_TQ = 1024
SB_TK = 256
SB_ND = SB_TQ // SB_TK
LOG2E = 1.4426950408889634
SB_SOFTPLUS_LINEAR = 64.0
SB_UNDERFLOW_LOG2 = -160.0


def _sb_body(q_ref, k_ref, v_ref, tri_ref, o_ref, acc_ref, c_ref, z_ref, w_ref, live_ref):
    i = pl.program_id(1)
    qi = i * SB_TQ
    acc_ref[...] = jnp.zeros_like(acc_ref)
    c_ref[...] = jnp.zeros_like(c_ref)

    def key_start(t):
        return pl.multiple_of(qi + (SB_ND - 1 - t) * SB_TK, SB_TK)

    def first_row(t):
        return max(0, SB_ND - 1 - t) * SB_TK if isinstance(t, int) else 0

    def p1(t, slot):
        r0 = first_row(t)
        kb = k_ref[pl.ds(key_start(t), SB_TK), :]
        z_ref[slot, r0:, :] = lax.dot_general(q_ref[r0:, :], kb, NT_DIMS, preferred_element_type=F32)

    def p23(t, slot, masked):
        r0 = first_row(t)
        z = z_ref[slot, r0:, :]
        sp = jnp.where(z > SB_SOFTPLUS_LINEAR, z, jnp.log2(1.0 + jnp.exp2(z)))
        c = c_ref[r0:, :]
        u0 = z - sp + jnp.concatenate([c, c], axis=1)
        if masked:
            shape = (SB_TQ - r0, SB_TK)
            row = lax.broadcasted_iota(jnp.int32, shape, 0) + r0
            col = lax.broadcasted_iota(jnp.int32, shape, 1)
            mask = (col + (SB_ND - 1 - t) * SB_TK) < row
            sp = jnp.where(mask, sp, 0.0)
            u0 = jnp.where(mask, u0, NEG)
        c_ref[r0:, :] = c - jnp.sum(sp, axis=-1, keepdims=True)
        live_ref[0] = (jnp.max(c_ref[r0:, :]) > SB_UNDERFLOW_LOG2).astype(jnp.int32)
        u = u0 + jnp.dot(sp.astype(BF16), tri_ref[...], preferred_element_type=F32)
        w_ref[slot, r0:, :] = jnp.exp2(u.astype(BF16))

    def p4(t, slot):
        r0 = first_row(t)
        vb = v_ref[pl.ds(key_start(t), SB_TK), :]
        acc_ref[r0:, :] += jnp.dot(w_ref[slot, r0:, :], vb, preferred_element_type=F32)

    n = (i + 1) * SB_ND

    def trip(it, slot, masked=False, stages=(True, True, True)):
        if stages[2]:
            p4(it - 2, slot)
        if stages[1]:
            p23(it - 1, 1 - slot, masked)
        if stages[0]:
            p1(it, slot)

    @pl.when(i == 0)
    def _():
        for t in range(SB_ND):
            p1(t, t % 2)
            p23(t, t % 2, True)
            p4(t, t % 2)

    @pl.when(i > 0)
    def _():
        for it in range(SB_ND + 1):
            trip(it, it % 2, True, (True, it >= 1, it >= 2))

        def cond(carry):
            it, live = carry
            return jnp.logical_and(it < n, live > 0)

        def body(carry):
            it, _ = carry
            for slot in range(2):
                @pl.when(it % 2 == slot)
                def _():
                    trip(it, slot)
            return it + 1, live_ref[0]

        it_end, live = lax.while_loop(cond, body, (jnp.int32(SB_ND + 1), live_ref[0]))
        for slot in range(2):
            @pl.when(it_end % 2 == slot)
            def _():
                p4(it_end - 2, slot)

            @pl.when(jnp.logical_and(it_end % 2 == slot, live > 0))
            def _():
                p23(it_end - 1, 1 - slot, False)
                p4(it_end - 1, 1 - slot)

    o_ref[...] = acc_ref[...].astype(o_ref.dtype)


def _stick_breaking(pm, tri):
    s = pm.shape[0]
    qb, kb, vb = OFF_SQ // 128, OFF_SK // 128, OFF_SV // 128
    return pl.pallas_call(
        _sb_body,
        grid=(SB_HEADS, s // SB_TQ),
        in_specs=[
            pl.BlockSpec((SB_TQ, 128), lambda h, i: (i, qb + h)),
            pl.BlockSpec((s, 128), lambda h, i: (0, kb + h)),
            pl.BlockSpec((s, 128), lambda h, i: (0, vb + h)),
            _const_spec((SB_TK, SB_TK), (0, 0)),
        ],
        out_specs=pl.BlockSpec((SB_TQ, 128), lambda h, i: (i, h)),
        out_shape=jax.ShapeDtypeStruct((s, SB_WIDTH), BF16),
        scratch_shapes=[pltpu.VMEM((SB_TQ, 128), F32), pltpu.VMEM((SB_TQ, 128), F32),
                        pltpu.VMEM((2, SB_TQ, SB_TK), F32), pltpu.VMEM((2, SB_TQ, SB_TK), BF16),
                        pltpu.SMEM((1,), jnp.int32)],
        compiler_params=_params(("parallel", "parallel")),
        name="stick_breaking",
    )(pm, pm, pm, tri)


RET_NC = 8


def _ret_body(q_ref, k_ref, v_ref, g_ref, cos_ref, sin_ref, perm_ref, dec_ref, zeta_ref, xi_ref,
              cd_ref, nw_ref, o_ref, r_ref, *, nc):
    n = pl.program_id(0)

    @pl.when(n == 0)
    def _():
        r_ref[...] = jnp.zeros_like(r_ref)

    cos = cos_ref[...]
    sin = sin_ref[...]
    perm = perm_ref[...]

    def rot(x):
        swapped = jnp.dot(x, perm, preferred_element_type=F32)
        return x.astype(F32) * cos + swapped * sin

    rq = rot(q_ref[...])
    rk = rot(k_ref[...])
    qb = rq.astype(BF16)
    qx = (rq * xi_ref[...]).astype(BF16)
    kb = rk.astype(BF16)
    kz = (rk * zeta_ref[...]).astype(BF16)
    lane = lax.broadcasted_iota(jnp.int32, (RET_CHUNK, 128), 1)
    lo = lane < RET_DK
    zero = jnp.zeros((RET_CHUNK, 128), BF16)
    for h in range(RET_HEADS):
        sl = slice((h // 2) * 128, (h // 2 + 1) * 128)
        sel = lo if h % 2 == 0 else jnp.logical_not(lo)
        hs = slice(h * RET_DV, (h + 1) * RET_DV)
        r = r_ref[h]
        for c in range(nc):
            rs = slice(c * RET_CHUNK, (c + 1) * RET_CHUNK)
            v = v_ref[rs, hs]
            qm = jnp.where(sel, qb[rs, sl], zero)
            qxm = jnp.where(sel, qx[rs, sl], zero)
            intra = lax.dot_general(qm, kb[rs, sl], NT_DIMS, preferred_element_type=F32) * dec_ref[h]
            o = jnp.dot(intra.astype(BF16), v, preferred_element_type=F32)
            o = o + jnp.dot(qxm, r.astype(BF16), preferred_element_type=F32)
            kv = lax.dot_general(kz[rs, sl], v, TN_DIMS, preferred_element_type=F32)
            r = r * cd_ref[h] + kv
            mu = jnp.mean(o, axis=-1, keepdims=True)
            d = o - mu
            var = jnp.mean(d * d, axis=-1, keepdims=True)
            on = (d * lax.rsqrt(var + EPS)) * nw_ref[:, hs]
            gate = g_ref[rs, hs].astype(F32)
            o_ref[rs, hs] = (gate * jax.nn.sigmoid(gate) * on).astype(o_ref.dtype)
        r_ref[h] = r


def _retention(pm, tabs, ret_nw3, l):
    s = pm.shape[0]
    cos_t, sin_t, perm, dec, zeta, xi, cd = tabs
    nc = min(RET_NC, s // RET_CHUNK)
    rows = nc * RET_CHUNK
    zeta_t, xi_t = jnp.tile(zeta, (nc, 1)), jnp.tile(xi, (nc, 1))
    return pl.pallas_call(
        functools.partial(_ret_body, nc=nc),
        grid=(s // rows,),
        in_specs=[
            pl.BlockSpec((rows, 256), lambda n: (n, OFF_RQ // 256)),
            pl.BlockSpec((rows, 256), lambda n: (n, OFF_RK // 256)),
            pl.BlockSpec((rows, 512), lambda n: (n, OFF_RV // 512)),
            pl.BlockSpec((rows, 512), lambda n: (n, OFF_RG // 512)),
            pl.BlockSpec((rows, 256), lambda n: (n, 0)),
            pl.BlockSpec((rows, 256), lambda n: (n, 0)),
            _const_spec((256, 256), (0, 0)),
            _const_spec((RET_HEADS, RET_CHUNK, RET_CHUNK), (0, 0, 0)),
            _const_spec((rows, 256), (0, 0)),
            _const_spec((rows, 256), (0, 0)),
            _const_spec((RET_HEADS, 128, RET_DV), (0, 0, 0)),
            _const_spec((None, 1, RET_V_WIDTH), (l, 0, 0)),
        ],
        out_specs=pl.BlockSpec((rows, RET_V_WIDTH), lambda n: (n, 0)),
        out_shape=jax.ShapeDtypeStruct((s, RET_V_WIDTH), BF16),
        scratch_shapes=[pltpu.VMEM((RET_HEADS, 128, RET_DV), F32)],
        compiler_params=_params(("arbitrary",)),
        name="retention",
    )(pm, pm, pm, pm, cos_t, sin_t, perm, dec, zeta_t, xi_t, cd, ret_nw3)


MG_TM = 512


def _merge_body(x_ref, o0, o1, o2, l0, l1, l2, yb_ref, yc_ref, gi_ref, bg_ref,
                wa_ref, wb_ref, wc_ref, wo_ref, out_ref, so1, so2, sl1, sl2):
    for src, dst, d in ((o1, so1, DIL_GROUPS[1][1]), (l1, sl1, DIL_GROUPS[1][1]),
                        (o2, so2, DIL_GROUPS[2][1]), (l2, sl2, DIL_GROUPS[2][1])):
        for r in range(d):
            for j in range(A_OUT // 128):
                col = r * A_OUT + j * 128
                dst[j, pl.ds(r, MG_TM // d, stride=d), :] = src[:, col:col + 128]

    def tok(ref):
        return jnp.concatenate([ref[j] for j in range(A_OUT // 128)], axis=1)

    a0, a1, a2 = l0[...], tok(sl1), tok(sl2)
    m = jnp.maximum(jnp.maximum(a0, a1), a2)
    e0, e1, e2 = jnp.exp(a0 - m), jnp.exp(a1 - m), jnp.exp(a2 - m)
    ya = (e0 * o0[...] + e1 * tok(so1) + e2 * tok(so2)) / (e0 + e1 + e2)
    pa = jnp.dot(ya.astype(BF16), wa_ref[...], preferred_element_type=F32)
    pb = jnp.dot(yb_ref[...], wb_ref[...], preferred_element_type=F32)
    pc = jnp.dot(yc_ref[...], wc_ref[...], preferred_element_type=F32)
    merged = None
    for b, pr in enumerate((pa, pb, pc)):
        gate = jax.nn.sigmoid(gi_ref[:, b * D_MODEL:(b + 1) * D_MODEL].astype(F32) + bg_ref[b:b + 1, :])
        merged = gate * pr if merged is None else merged + gate * pr
    out_ref[...] = x_ref[...] + jnp.dot(merged.astype(BF16), wo_ref[...], preferred_element_type=F32)


def _merge(x, oas, lses, yb, yc, gi, b_gate, wa, wb, wc, wo, l):
    s = x.shape[0]
    tm = MG_TM
    row = lambda w: pl.BlockSpec((tm, w), lambda i: (i, 0))
    grp = [pl.BlockSpec((tm // d, d * A_OUT), lambda i: (i, 0)) for _, d in DIL_GROUPS]
    return pl.pallas_call(
        _merge_body,
        grid=(s // tm,),
        scratch_shapes=[pltpu.VMEM((A_OUT // 128, tm, 128), F32)] * 4,
        in_specs=[row(D_MODEL)] + grp + grp + [row(SB_WIDTH), row(RET_V_WIDTH), row(GATE_W),
                  _const_spec((None, N_BRANCH, D_MODEL), (l, 0, 0)),
                  _const_spec((None, A_OUT, D_MODEL), (l, 0, 0)),
                  _const_spec((None, SB_WIDTH, D_MODEL), (l, 0, 0)),
                  _const_spec((None, RET_V_WIDTH, D_MODEL), (l, 0, 0)),
                  _const_spec((None, D_MODEL, D_MODEL), (l, 0, 0))],
        out_specs=row(D_MODEL),
        out_shape=jax.ShapeDtypeStruct((s, D_MODEL), F32),
        compiler_params=_params(("parallel",)),
        name="merge_outproj",
    )(x, *oas, *lses, yb, yc, gi, b_gate, wa, wb, wc, wo)


FF_TM = 512
FF_CH = 256
CARRY = 8


def _gelu_tanh(x):
    return 0.5 * x * (1.0 + jnp.tanh(np.sqrt(2.0 / np.pi).astype(np.float32) * (x + 0.044715 * (x * x * x))))


def _rms(x, w):
    return x * lax.rsqrt(jnp.mean(x * x, axis=-1, keepdims=True) + EPS) * w


def _ffn_body(x_ref, nw_ref, wu_ref, wg_ref, cw_ref, cb_ref, wd_ref, fw_ref, out_ref, u_ref, a_ref, *, final):
    i = pl.program_id(0)

    @pl.when(i == 0)
    def _():
        u_ref[0:CARRY, :] = jnp.zeros((CARRY, D_FF), F32)

    x = x_ref[...]
    h = (x * lax.rsqrt(jnp.mean(x * x, axis=-1, keepdims=True) + EPS) * nw_ref[...]).astype(BF16)
    for c in range(D_FF // FF_CH):
        sl = slice(c * FF_CH, (c + 1) * FF_CH)
        u_ref[CARRY:CARRY + FF_TM, sl] = jnp.dot(h, wu_ref[:, sl], preferred_element_type=F32)
        u0 = u_ref[CARRY:CARRY + FF_TM, sl]
        u1 = u_ref[CARRY - 1:CARRY - 1 + FF_TM, sl]
        u2 = u_ref[CARRY - 2:CARRY - 2 + FF_TM, sl]
        uc = cw_ref[0:1, sl] * u2 + cw_ref[1:2, sl] * u1 + cw_ref[2:3, sl] * u0 + cb_ref[:, sl]
        g = jnp.dot(h, wg_ref[:, sl], preferred_element_type=F32)
        a_ref[:, sl] = (_gelu_tanh(uc) * g).astype(BF16)
    u_ref[0:CARRY, :] = u_ref[FF_TM:FF_TM + CARRY, :]
    y = x + jnp.dot(a_ref[...], wd_ref[...], preferred_element_type=F32)
    out_ref[...] = _rms(y, fw_ref[...]) if final else y


def _conv_ffn(x, nw3, wu, wg, cw, cb3, wd, fw, l):
    s = x.shape[0]
    tm = FF_TM
    return pl.pallas_call(
        functools.partial(_ffn_body, final=(l == DEPTH - 1)),
        grid=(s // tm,),
        in_specs=[
            pl.BlockSpec((tm, D_MODEL), lambda i: (i, 0)),
            _const_spec((None, 1, D_MODEL), (l, 0, 0)),
            _const_spec((None, D_MODEL, D_FF), (l, 0, 0)),
            _const_spec((None, D_MODEL, D_FF), (l, 0, 0)),
            _const_spec((None, 3, D_FF), (l, 0, 0)),
            _const_spec((None, 1, D_FF), (l, 0, 0)),
            _const_spec((None, D_FF, D_MODEL), (l, 0, 0)),
            _const_spec((1, D_MODEL), (0, 0)),
        ],
        out_specs=pl.BlockSpec((tm, D_MODEL), lambda i: (i, 0)),
        out_shape=jax.ShapeDtypeStruct((s, D_MODEL), F32),
        scratch_shapes=[pltpu.VMEM((tm + CARRY, D_FF), F32), pltpu.VMEM((tm, D_FF), BF16)],
        compiler_params=_params(("arbitrary",)),
        name="conv_ffn",
    )(x, nw3, wu, wg, cw, cb3, wd, fw)


def _t5_causal_bucket(dist):
    max_exact = REL_BUCKETS // 2
    n = np.maximum(dist, 0)
    large = max_exact + (np.log(np.maximum(n, 1) / max_exact) / np.log(REL_MAX_DIST / max_exact)
                         * (REL_BUCKETS - max_exact)).astype(np.int64)
    large = np.minimum(large, REL_BUCKETS - 1)
    return np.where(n < max_exact, n, large).astype(np.int32)


def _dilated_bias(rel_bias, g, window, dilation):
    n_steps = window // dilation
    steps = np.arange(BLOCK)[:, None] + BLOCK - np.arange(2 * BLOCK)[None, :]
    in_band = (steps >= 0) & (steps <= n_steps)
    bucket = _t5_causal_bucket(steps * dilation)
    bias_g = rel_bias[:, g * A_HEADS_PER_GROUP:(g + 1) * A_HEADS_PER_GROUP].astype(F32)
    onehot = (jnp.asarray(bucket)[:, :, None] == jnp.arange(REL_BUCKETS)[None, None, :]).astype(F32)
    bias = jnp.einsum('qkb,bh->hqk', onehot, bias_g, precision=lax.Precision.HIGHEST)
    return jnp.where(jnp.asarray(in_band)[None], bias, NEG)


def _retention_tables(s):
    half = RET_DK // 2
    inv = ROPE_BASE ** (-jnp.arange(half, dtype=F32) / half)
    ang = jnp.arange(s, dtype=jnp.int32).astype(F32)[:, None] * inv[None, :]
    cos, sin = jnp.cos(ang), jnp.sin(ang)
    cos_t = jnp.tile(jnp.concatenate([cos, cos], axis=-1), (1, RET_HEADS))
    sin_t = jnp.tile(jnp.concatenate([-sin, sin], axis=-1), (1, RET_HEADS))
    j = np.arange(RET_QK_WIDTH)
    partner = np.where((j % RET_DK) < half, j + half, j - half)
    perm = np.zeros((RET_QK_WIDTH, RET_QK_WIDTH), np.float32)
    perm[partner, j] = 1.0
    log_gamma = jnp.log1p(-jnp.exp2(-5.0 - jnp.arange(RET_HEADS, dtype=F32)))
    n = jnp.arange(RET_CHUNK, dtype=F32)
    diff = n[:, None] - n[None, :]
    dec = jnp.where(diff >= 0, jnp.exp(diff[None] * log_gamma[:, None, None]), 0.0)
    zeta = jnp.exp((RET_CHUNK - 1 - n)[:, None] * log_gamma[None, :])
    xi = jnp.exp((n + 1)[:, None] * log_gamma[None, :])
    zeta = jnp.repeat(zeta, RET_DK, axis=1)
    xi = jnp.repeat(xi, RET_DK, axis=1)
    cd = jnp.broadcast_to(jnp.exp(RET_CHUNK * log_gamma)[:, None, None], (RET_HEADS, 128, RET_DV))
    return cos_t, sin_t, jnp.asarray(perm, BF16), dec, zeta, xi, cd


def _scale_row():
    sc = np.ones((1, MAIN_W), np.float32)
    sc[:, 0:A_WIDTH] = HEAD_DIM ** -0.5
    sc[:, A_W + OFF_SQ:A_W + OFF_SQ + SB_WIDTH] = SB_HEAD_DIM ** -0.5 * LOG2E
    sc[:, A_W + OFF_RK:A_W + OFF_RK + RET_QK_WIDTH] = RET_DK ** -0.5
    return jnp.asarray(sc)


def _tri_table():
    k = np.arange(SB_TK)[:, None]
    j = np.arange(SB_TK)[None, :]
    return jnp.asarray(-(k > j).astype(np.float32), BF16)


def kernel(x, rel_bias, norm_mix_w, w_in, b_gate, ret_norm_w, w_proj_a, w_proj_b, w_proj_c, w_out,
           norm_ffn_w, w_up, w_gate, conv_w, conv_b, w_down, final_norm_w):
    b, s, _ = x.shape
    assert b == 1
    xs = x.reshape(s, D_MODEL)
    w_in_bf = w_in.astype(BF16)
    wa, wb, wc, wo = (w.astype(BF16) for w in (w_proj_a, w_proj_b, w_proj_c, w_out))
    wu, wg, wd = (w.astype(BF16) for w in (w_up, w_gate, w_down))
    nmix3 = norm_mix_w.reshape(DEPTH, 1, D_MODEL)
    nffn3 = norm_ffn_w.reshape(DEPTH, 1, D_MODEL)
    rnw3 = ret_norm_w.reshape(DEPTH, 1, RET_V_WIDTH)
    cb3 = conv_b.reshape(DEPTH, 1, D_FF)
    scale_row = _scale_row()
    tri = _tri_table()
    biases = [_dilated_bias(rel_bias, g, w, d) for g, (w, d) in enumerate(DIL_GROUPS)]
    rtabs = _retention_tables(s)

    for l in range(DEPTH):
        a0, a1, a2, pm, gi = _inproj(xs, nmix3, scale_row, w_in_bf, l)
        oas, lses = [], []
        for g, (a_g, (_, dilation)) in enumerate(zip((a0, a1, a2), DIL_GROUPS)):
            o_g, lse_g = _dilated_group(a_g, biases[g], dilation)
            oas.append(o_g)
            lses.append(lse_g)
        yb = _stick_breaking(pm, tri)
        yc = _retention(pm, rtabs, rnw3, l)
        xs = _merge(xs, oas, lses, yb, yc, gi, b_gate, wa, wb, wc, wo, l)
        xs = _conv_ffn(xs, nffn3, wu, wg, conv_w, cb3, wd, final_norm_w.reshape(1, D_MODEL), l)
    return xs.reshape(b, s, D_MODEL)
```

```python
import functools

import numpy as np
import jax
import jax.numpy as jnp
from jax import lax
from jax.experimental import pallas as pl
from jax.experimental.pallas import tpu as pltpu

F32 = jnp.float32
BF16 = jnp.bfloat16

D_MODEL = 1024
DEPTH = 4
HEAD_DIM = 64
BLOCK = 128
DIL_GROUPS = ((128, 1), (512, 4), (2048, 16))
N_GROUPS = 3
A_HEADS_PER_GROUP = 4
A_WIDTH = 768
A_OUT = 256
SB_HEADS = 4
SB_HEAD_DIM = 128
SB_WIDTH = 512
RET_HEADS = 4
RET_DK = 64
RET_DV = 128
RET_QK_WIDTH = 256
RET_V_WIDTH = 512
RET_CHUNK = 128
ROPE_BASE = 10000.0
REL_BUCKETS = 32
REL_MAX_DIST = 2048
N_BRANCH = 3
D_FF = 2816
EPS = 1e-6

GATE_W = N_BRANCH * D_MODEL
A_W = 3 * A_WIDTH
PM_W = 3 * SB_WIDTH + 2 * RET_QK_WIDTH + 2 * RET_V_WIDTH
MAIN_W = A_W + PM_W
IN_WIDTH = MAIN_W + GATE_W
G_W = 3 * A_OUT
OFF_SQ, OFF_SK, OFF_SV = 0, 512, 1024
OFF_RQ, OFF_RK, OFF_RV, OFF_RG = 1536, 1792, 2048, 2560

NEG = -1e30
VMEM_LIMIT = 56 * 1024 * 1024

NT_DIMS = (((1,), (1,)), ((), ()))
TN_DIMS = (((0,), (0,)), ((), ()))


def _params(sem, vmem=VMEM_LIMIT):
    return pltpu.CompilerParams(dimension_semantics=sem, vmem_limit_bytes=vmem)


def _const_spec(shape, index):
    nd = len(index)
    return pl.BlockSpec(shape, lambda *_: index, pipeline_mode=pl.Buffered(1))


IN_TM = 512
IN_CH = 768


def _inproj_body(x_ref, nw_ref, sc_ref, w_ref, a0_ref, a1_ref, a2_ref, pm_ref, gi_ref, ys_ref):
    x = x_ref[...]
    h = (x * lax.rsqrt(jnp.mean(x * x, axis=-1, keepdims=True) + EPS) * nw_ref[...]).astype(BF16)
    a_refs = (a0_ref, a1_ref, a2_ref)
    for c in range(A_W // IN_CH):
        sl = slice(c * IN_CH, (c + 1) * IN_CH)
        y = jnp.dot(h, w_ref[:, sl], preferred_element_type=F32) * sc_ref[:, sl]
        for j in range(IN_CH // 128):
            ys_ref[j] = y[:, j * 128:(j + 1) * 128]
        for g, (_, d) in enumerate(DIL_GROUPS):
            for r in range(d):
                rows = pl.ds(r, IN_TM // d, stride=d) if d > 1 else slice(None)
                for j in range(A_OUT // 128):
                    col = r * G_W + c * A_OUT + j * 128
                    a_refs[g][:, col:col + 128] = ys_ref[g * (A_OUT // 128) + j, rows, :].astype(BF16)
    for c in range(PM_W // IN_CH):
        sl = slice(A_W + c * IN_CH, A_W + (c + 1) * IN_CH)
        y = jnp.dot(h, w_ref[:, sl], preferred_element_type=F32)
        pm_ref[:, c * IN_CH:(c + 1) * IN_CH] = (y * sc_ref[:, sl]).astype(BF16)
    for c in range(GATE_W // IN_CH):
        y = jnp.dot(h, w_ref[:, MAIN_W + c * IN_CH:MAIN_W + (c + 1) * IN_CH], preferred_element_type=F32)
        gi_ref[:, c * IN_CH:(c + 1) * IN_CH] = y.astype(BF16)


def _inproj(x, norm_w3, scale_row, w_in_bf, l):
    s = x.shape[0]
    a_shapes = [jax.ShapeDtypeStruct((s // d, d * G_W), BF16) for _, d in DIL_GROUPS]
    a_specs = [pl.BlockSpec((IN_TM // d, d * G_W), lambda i: (i, 0)) for _, d in DIL_GROUPS]
    return pl.pallas_call(
        _inproj_body,
        grid=(s // IN_TM,),
        in_specs=[
            pl.BlockSpec((IN_TM, D_MODEL), lambda i: (i, 0)),
            _const_spec((None, 1, D_MODEL), (l, 0, 0)),
            _const_spec((1, MAIN_W), (0, 0)),
            _const_spec((None, D_MODEL, IN_WIDTH), (l, 0, 0)),
        ],
        out_specs=a_specs + [
            pl.BlockSpec((IN_TM, PM_W), lambda i: (i, 0)),
            pl.BlockSpec((IN_TM, GATE_W), lambda i: (i, 0)),
        ],
        out_shape=a_shapes + [jax.ShapeDtypeStruct((s, PM_W), BF16), jax.ShapeDtypeStruct((s, GATE_W), BF16)],
        scratch_shapes=[pltpu.VMEM((IN_CH // 128, IN_TM, 128), F32)],
        compiler_params=_params(("parallel",)),
        name="inproj",
    )(x, norm_w3, scale_row, w_in_bf)


DIL_NB = 8


def _dil_body(q_ref, k_ref, ke_ref, v_ref, ve_ref, bias_ref, o_ref, lse_ref, *, nblk):
    n = pl.program_id(1)
    lane = lax.broadcasted_iota(jnp.int32, (BLOCK, 128), 1)
    lo = lane < HEAD_DIM
    col = lax.broadcasted_iota(jnp.int32, (BLOCK, 2 * BLOCK), 1)
    first_ok = jnp.logical_or(n > 0, col >= BLOCK)
    for b in range(nblk):
        for p in range(2):
            sl = slice(p * 128, (p + 1) * 128)
            q = q_ref[b * BLOCK:(b + 1) * BLOCK, sl]
            if b == 0:
                k2 = jnp.concatenate([ke_ref[:, sl], k_ref[0:BLOCK, sl]], axis=0)
                v2 = jnp.concatenate([ve_ref[:, sl], v_ref[0:BLOCK, sl]], axis=0)
            else:
                k2 = k_ref[(b - 1) * BLOCK:(b + 1) * BLOCK, sl]
                v2 = v_ref[(b - 1) * BLOCK:(b + 1) * BLOCK, sl]
            outs, lses = [], []
            for hh in range(2):
                h = 2 * p + hh
                sel = lo if hh == 0 else jnp.logical_not(lo)
                qm = jnp.where(sel, q, jnp.zeros_like(q))
                s = lax.dot_general(qm, k2, NT_DIMS, preferred_element_type=F32) + bias_ref[h]
                if b == 0:
                    s = jnp.where(first_ok, s, NEG)
                m = jnp.max(s, axis=-1, keepdims=True)
                e = jnp.exp(s - m)
                den = jnp.sum(e, axis=-1, keepdims=True)
                num = jnp.dot(e.astype(BF16), v2, preferred_element_type=F32)
                outs.append(num / den)
                lses.append(jnp.broadcast_to(m + jnp.log(den), (BLOCK, 128)))
            o_ref[b * BLOCK:(b + 1) * BLOCK, sl] = jnp.where(lo, outs[0], outs[1])
            lse_ref[b * BLOCK:(b + 1) * BLOCK, sl] = jnp.where(lo, lses[0], lses[1])


def _dilated_group(a_g, bias_g, dilation):
    rows = a_g.shape[0]
    nblk = min(DIL_NB, rows // BLOCK)
    tr = nblk * BLOCK
    cb = G_W // A_OUT

    def cur(off):
        return pl.BlockSpec((tr, A_OUT), lambda r, n: (n, r * cb + off))

    def edge(off):
        return pl.BlockSpec((BLOCK, A_OUT), lambda r, n: (jnp.maximum(n * nblk - 1, 0), r * cb + off))

    return pl.pallas_call(
        functools.partial(_dil_body, nblk=nblk),
        grid=(dilation, rows // tr),
        in_specs=[cur(0), cur(1), edge(1), cur(2), edge(2),
                  _const_spec((A_HEADS_PER_GROUP, BLOCK, 2 * BLOCK), (0, 0, 0))],
        out_specs=[pl.BlockSpec((tr, A_OUT), lambda r, n: (n, r)),
                   pl.BlockSpec((tr, A_OUT), lambda r, n: (n, r))],
        out_shape=[jax.ShapeDtypeStruct((rows, dilation * A_OUT), F32),
                   jax.ShapeDtypeStruct((rows, dilation * A_OUT), F32)],
        compiler_params=_params(("parallel", "parallel")),
        name=f"dilated_attn_d{dilation}",
    )(a_g, a_g, a_g, a_g, a_g, bias_g)


SB_TQ = 1024
SB_TK = 256
SB_ND = SB_TQ // SB_TK
LOG2E = 1.4426950408889634
SB_SOFTPLUS_LINEAR = 64.0
SB_UNDERFLOW_LOG2 = -160.0


def _sb_body(q_ref, k_ref, v_ref, tri_ref, o_ref, acc_ref, c_ref, z_ref, w_ref, live_ref):
    i = pl.program_id(1)
    qi = i * SB_TQ
    acc_ref[...] = jnp.zeros_like(acc_ref)
    c_ref[...] = jnp.zeros_like(c_ref)

    def key_start(t):
        return pl.multiple_of(qi + (SB_ND - 1 - t) * SB_TK, SB_TK)

    def first_row(t):
        return max(0, SB_ND - 1 - t) * SB_TK if isinstance(t, int) else 0

    def p1(t, slot):
        r0 = first_row(t)
        kb = k_ref[pl.ds(key_start(t), SB_TK), :]
        z_ref[slot, r0:, :] = lax.dot_general(q_ref[r0:, :], kb, NT_DIMS, preferred_element_type=F32)

    def p23(t, slot, masked):
        r0 = first_row(t)
        z = z_ref[slot, r0:, :]
        sp = jnp.where(z > SB_SOFTPLUS_LINEAR, z, jnp.log2(1.0 + jnp.exp2(z)))
        c = c_ref[r0:, :]
        u0 = z - sp + jnp.concatenate([c, c], axis=1)
        if masked:
            shape = (SB_TQ - r0, SB_TK)
            row = lax.broadcasted_iota(jnp.int32, shape, 0) + r0
            col = lax.broadcasted_iota(jnp.int32, shape, 1)
            mask = (col + (SB_ND - 1 - t) * SB_TK) < row
            sp = jnp.where(mask, sp, 0.0)
            u0 = jnp.where(mask, u0, NEG)
        c_ref[r0:, :] = c - jnp.sum(sp, axis=-1, keepdims=True)
        live_ref[0] = (jnp.max(c_ref[r0:, :]) > SB_UNDERFLOW_LOG2).astype(jnp.int32)
        u = u0 + jnp.dot(sp.astype(BF16), tri_ref[...], preferred_element_type=F32)
        w_ref[slot, r0:, :] = jnp.exp2(u.astype(BF16))

    def p4(t, slot):
        r0 = first_row(t)
        vb = v_ref[pl.ds(key_start(t), SB_TK), :]
        acc_ref[r0:, :] += jnp.dot(w_ref[slot, r0:, :], vb, preferred_element_type=F32)

    n = (i + 1) * SB_ND

    def trip(it, slot, masked=False, stages=(True, True, True)):
        if stages[2]:
            p4(it - 2, slot)
        if stages[1]:
            p23(it - 1, 1 - slot, masked)
        if stages[0]:
            p1(it, slot)

    @pl.when(i == 0)
    def _():
        for t in range(SB_ND):
            p1(t, t % 2)
            p23(t, t % 2, True)
            p4(t, t % 2)

    @pl.when(i > 0)
    def _():
        for it in range(SB_ND + 1):
            trip(it, it % 2, True, (True, it >= 1, it >= 2))

        def cond(carry):
            it, live = carry
            return jnp.logical_and(it < n, live > 0)

        def body(carry):
            it, _ = carry
            for slot in range(2):
                @pl.when(it % 2 == slot)
                def _():
                    trip(it, slot)
            return it + 1, live_ref[0]

        it_end, live = lax.while_loop(cond, body, (jnp.int32(SB_ND + 1), live_ref[0]))
        for slot in range(2):
            @pl.when(it_end % 2 == slot)
            def _():
                p4(it_end - 2, slot)

            @pl.when(jnp.logical_and(it_end % 2 == slot, live > 0))
            def _():
                p23(it_end - 1, 1 - slot, False)
                p4(it_end - 1, 1 - slot)

    o_ref[...] = acc_ref[...].astype(o_ref.dtype)


def _stick_breaking(pm, tri):
    s = pm.shape[0]
    qb, kb, vb = OFF_SQ // 128, OFF_SK // 128, OFF_SV // 128
    return pl.pallas_call(
        _sb_body,
        grid=(SB_HEADS, s // SB_TQ),
        in_specs=[
            pl.BlockSpec((SB_TQ, 128), lambda h, i: (i, qb + h)),
            pl.BlockSpec((s, 128), lambda h, i: (0, kb + h)),
            pl.BlockSpec((s, 128), lambda h, i: (0, vb + h)),
            _const_spec((SB_TK, SB_TK), (0, 0)),
        ],
        out_specs=pl.BlockSpec((SB_TQ, 128), lambda h, i: (i, h)),
        out_shape=jax.ShapeDtypeStruct((s, SB_WIDTH), BF16),
        scratch_shapes=[pltpu.VMEM((SB_TQ, 128), F32), pltpu.VMEM((SB_TQ, 128), F32),
                        pltpu.VMEM((2, SB_TQ, SB_TK), F32), pltpu.VMEM((2, SB_TQ, SB_TK), BF16),
                        pltpu.SMEM((1,), jnp.int32)],
        compiler_params=_params(("parallel", "parallel")),
        name="stick_breaking",
    )(pm, pm, pm, tri)


RET_NC = 8


def _ret_body(q_ref, k_ref, v_ref, g_ref, cos_ref, sin_ref, perm_ref, dec_ref, zeta_ref, xi_ref,
              cd_ref, nw_ref, o_ref, r_ref, *, nc):
    n = pl.program_id(0)

    @pl.when(n == 0)
    def _():
        r_ref[...] = jnp.zeros_like(r_ref)

    cos = cos_ref[...]
    sin = sin_ref[...]
    perm = perm_ref[...]

    def rot(x):
        swapped = jnp.dot(x, perm, preferred_element_type=F32)
        return x.astype(F32) * cos + swapped * sin

    rq = rot(q_ref[...])
    rk = rot(k_ref[...])
    qb = rq.astype(BF16)
    qx = (rq * xi_ref[...]).astype(BF16)
    kb = rk.astype(BF16)
    kz = (rk * zeta_ref[...]).astype(BF16)
    lane = lax.broadcasted_iota(jnp.int32, (RET_CHUNK, 128), 1)
    lo = lane < RET_DK
    zero = jnp.zeros((RET_CHUNK, 128), BF16)
    for h in range(RET_HEADS):
        sl = slice((h // 2) * 128, (h // 2 + 1) * 128)
        sel = lo if h % 2 == 0 else jnp.logical_not(lo)
        hs = slice(h * RET_DV, (h + 1) * RET_DV)
        r = r_ref[h]
        for c in range(nc):
            rs = slice(c * RET_CHUNK, (c + 1) * RET_CHUNK)
            v = v_ref[rs, hs]
            qm = jnp.where(sel, qb[rs, sl], zero)
            qxm = jnp.where(sel, qx[rs, sl], zero)
            intra = lax.dot_general(qm, kb[rs, sl], NT_DIMS, preferred_element_type=F32) * dec_ref[h]
            o = jnp.dot(intra.astype(BF16), v, preferred_element_type=F32)
            o = o + jnp.dot(qxm, r.astype(BF16), preferred_element_type=F32)
            kv = lax.dot_general(kz[rs, sl], v, TN_DIMS, preferred_element_type=F32)
            r = r * cd_ref[h] + kv
            mu = jnp.mean(o, axis=-1, keepdims=True)
            d = o - mu
            var = jnp.mean(d * d, axis=-1, keepdims=True)
            on = (d * lax.rsqrt(var + EPS)) * nw_ref[:, hs]
            gate = g_ref[rs, hs].astype(F32)
            o_ref[rs, hs] = (gate * jax.nn.sigmoid(gate) * on).astype(o_ref.dtype)
        r_ref[h] = r


def _retention(pm, tabs, ret_nw3, l):
    s = pm.shape[0]
    cos_t, sin_t, perm, dec, zeta, xi, cd = tabs
    nc = min(RET_NC, s // RET_CHUNK)
    rows = nc * RET_CHUNK
    zeta_t, xi_t = jnp.tile(zeta, (nc, 1)), jnp.tile(xi, (nc, 1))
    return pl.pallas_call(
        functools.partial(_ret_body, nc=nc),
        grid=(s // rows,),
        in_specs=[
            pl.BlockSpec((rows, 256), lambda n: (n, OFF_RQ // 256)),
            pl.BlockSpec((rows, 256), lambda n: (n, OFF_RK // 256)),
            pl.BlockSpec((rows, 512), lambda n: (n, OFF_RV // 512)),
            pl.BlockSpec((rows, 512), lambda n: (n, OFF_RG // 512)),
            pl.BlockSpec((rows, 256), lambda n: (n, 0)),
            pl.BlockSpec((rows, 256), lambda n: (n, 0)),
            _const_spec((256, 256), (0, 0)),
            _const_spec((RET_HEADS, RET_CHUNK, RET_CHUNK), (0, 0, 0)),
            _const_spec((rows, 256), (0, 0)),
            _const_spec((rows, 256), (0, 0)),
            _const_spec((RET_HEADS, 128, RET_DV), (0, 0, 0)),
            _const_spec((None, 1, RET_V_WIDTH), (l, 0, 0)),
        ],
        out_specs=pl.BlockSpec((rows, RET_V_WIDTH), lambda n: (n, 0)),
        out_shape=jax.ShapeDtypeStruct((s, RET_V_WIDTH), BF16),
        scratch_shapes=[pltpu.VMEM((RET_HEADS, 128, RET_DV), F32)],
        compiler_params=_params(("arbitrary",)),
        name="retention",
    )(pm, pm, pm, pm, cos_t, sin_t, perm, dec, zeta_t, xi_t, cd, ret_nw3)


MF_TM = 512
FF_CH = 256
CARRY = 8
MF_VMEM_LIMIT = 60 * 1024 * 1024


def _gelu_tanh(x):
    return 0.5 * x * (1.0 + jnp.tanh(np.sqrt(2.0 / np.pi).astype(np.float32) * (x + 0.044715 * (x * x * x))))


def _rms(x, w):
    return x * lax.rsqrt(jnp.mean(x * x, axis=-1, keepdims=True) + EPS) * w


def _mix_ffn_body(x_ref, o0, o1, o2, l0, l1, l2, yb_ref, yc_ref, gi_ref, bg_ref, wa_ref, wb_ref, wc_ref, wo_ref,
                  nw_ref, wu_ref, wg_ref, cw_ref, cb_ref, wd_ref, fw_ref, out_ref,
                  so1, so2, sl1, sl2, u_ref, carry_ref, a_ref, *, final):
    i = pl.program_id(0)

    @pl.when(i == 0)
    def _():
        carry_ref[...] = jnp.zeros_like(carry_ref)

    for src, dst, d in ((o1, so1, DIL_GROUPS[1][1]), (l1, sl1, DIL_GROUPS[1][1]),
                        (o2, so2, DIL_GROUPS[2][1]), (l2, sl2, DIL_GROUPS[2][1])):
        for r in range(d):
            for j in range(A_OUT // 128):
                col = r * A_OUT + j * 128
                dst[j, pl.ds(r, MF_TM // d, stride=d), :] = src[:, col:col + 128]

    def tok(ref):
        return jnp.concatenate([ref[j] for j in range(A_OUT // 128)], axis=1)

    a0, a1, a2 = l0[...], tok(sl1), tok(sl2)
    m = jnp.maximum(jnp.maximum(a0, a1), a2)
    e0, e1, e2 = jnp.exp(a0 - m), jnp.exp(a1 - m), jnp.exp(a2 - m)
    ya = (e0 * o0[...] + e1 * tok(so1) + e2 * tok(so2)) / (e0 + e1 + e2)
    pa = jnp.dot(ya.astype(BF16), wa_ref[...], preferred_element_type=F32)
    pb = jnp.dot(yb_ref[...], wb_ref[...], preferred_element_type=F32)
    pc = jnp.dot(yc_ref[...], wc_ref[...], preferred_element_type=F32)
    merged = None
    for b, pr in enumerate((pa, pb, pc)):
        gate = jax.nn.sigmoid(gi_ref[:, b * D_MODEL:(b + 1) * D_MODEL].astype(F32) + bg_ref[b:b + 1, :])
        merged = gate * pr if merged is None else merged + gate * pr
    x = x_ref[...] + jnp.dot(merged.astype(BF16), wo_ref[...], preferred_element_type=F32)

    h = _rms(x, nw_ref[...]).astype(BF16)
    for c in range(D_FF // FF_CH):
        sl = slice(c * FF_CH, (c + 1) * FF_CH)
        ub = u_ref.at[c % 2]
        ub[0:CARRY, :] = carry_ref[:, sl]
        ub[CARRY:CARRY + MF_TM, :] = jnp.dot(h, wu_ref[:, sl], preferred_element_type=F32)
        u0 = ub[CARRY:CARRY + MF_TM, :]
        u1 = ub[CARRY - 1:CARRY - 1 + MF_TM, :]
        u2 = ub[CARRY - 2:CARRY - 2 + MF_TM, :]
        carry_ref[:, sl] = ub[MF_TM:MF_TM + CARRY, :]
        uc = cw_ref[0:1, sl] * u2 + cw_ref[1:2, sl] * u1 + cw_ref[2:3, sl] * u0 + cb_ref[:, sl]
        g = jnp.dot(h, wg_ref[:, sl], preferred_element_type=F32)
        a_ref[:, sl] = (_gelu_tanh(uc) * g).astype(BF16)
    y = x + jnp.dot(a_ref[...], wd_ref[...], preferred_element_type=F32)
    out_ref[...] = _rms(y, fw_ref[...]) if final else y


def _mix_ffn(x, oas, lses, yb, yc, gi, b_gate, wa, wb, wc, wo, nw3, wu, wg, cw, cb3, wd, fw, l):
    s = x.shape[0]
    tm = MF_TM
    row = lambda w: pl.BlockSpec((tm, w), lambda i: (i, 0))
    grp = [pl.BlockSpec((tm // d, d * A_OUT), lambda i: (i, 0)) for _, d in DIL_GROUPS]
    return pl.pallas_call(
        functools.partial(_mix_ffn_body, final=(l == DEPTH - 1)),
        grid=(s // tm,),
        in_specs=[row(D_MODEL)] + grp + grp + [row(SB_WIDTH), row(RET_V_WIDTH), row(GATE_W),
                  _const_spec((None, N_BRANCH, D_MODEL), (l, 0, 0)),
                  _const_spec((None, A_OUT, D_MODEL), (l, 0, 0)),
                  _const_spec((None, SB_WIDTH, D_MODEL), (l, 0, 0)),
                  _const_spec((None, RET_V_WIDTH, D_MODEL), (l, 0, 0)),
                  _const_spec((None, D_MODEL, D_MODEL), (l, 0, 0)),
                  _const_spec((None, 1, D_MODEL), (l, 0, 0)),
                  _const_spec((None, D_MODEL, D_FF), (l, 0, 0)),
                  _const_spec((None, D_MODEL, D_FF), (l, 0, 0)),
                  _const_spec((None, 3, D_FF), (l, 0, 0)),
                  _const_spec((None, 1, D_FF), (l, 0, 0)),
                  _const_spec((None, D_FF, D_MODEL), (l, 0, 0)),
                  _const_spec((1, D_MODEL), (0, 0))],
        out_specs=row(D_MODEL),
        out_shape=jax.ShapeDtypeStruct((s, D_MODEL), F32),
        scratch_shapes=[pltpu.VMEM((A_OUT // 128, tm, 128), F32)] * 4 + [
            pltpu.VMEM((2, tm + CARRY, FF_CH), F32), pltpu.VMEM((CARRY, D_FF), F32),
            pltpu.VMEM((tm, D_FF), BF16)],
        compiler_params=_params(("arbitrary",), MF_VMEM_LIMIT),
        name="mix_ffn",
    )(x, *oas, *lses, yb, yc, gi, b_gate, wa, wb, wc, wo, nw3, wu, wg, cw, cb3, wd, fw)


def _t5_causal_bucket(dist):
    max_exact = REL_BUCKETS // 2
    n = np.maximum(dist, 0)
    large = max_exact + (np.log(np.maximum(n, 1) / max_exact) / np.log(REL_MAX_DIST / max_exact)
                         * (REL_BUCKETS - max_exact)).astype(np.int64)
    large = np.minimum(large, REL_BUCKETS - 1)
    return np.where(n < max_exact, n, large).astype(np.int32)


def _dilated_bias(rel_bias, g, window, dilation):
    n_steps = window // dilation
    steps = np.arange(BLOCK)[:, None] + BLOCK - np.arange(2 * BLOCK)[None, :]
    in_band = (steps >= 0) & (steps <= n_steps)
    bucket = _t5_causal_bucket(steps * dilation)
    bias_g = rel_bias[:, g * A_HEADS_PER_GROUP:(g + 1) * A_HEADS_PER_GROUP].astype(F32)
    onehot = (jnp.asarray(bucket)[:, :, None] == jnp.arange(REL_BUCKETS)[None, None, :]).astype(F32)
    bias = jnp.einsum('qkb,bh->hqk', onehot, bias_g, precision=lax.Precision.HIGHEST)
    return jnp.where(jnp.asarray(in_band)[None], bias, NEG)


def _retention_tables(s):
    half = RET_DK // 2
    inv = ROPE_BASE ** (-jnp.arange(half, dtype=F32) / half)
    ang = jnp.arange(s, dtype=jnp.int32).astype(F32)[:, None] * inv[None, :]
    cos, sin = jnp.cos(ang), jnp.sin(ang)
    cos_t = jnp.tile(jnp.concatenate([cos, cos], axis=-1), (1, RET_HEADS))
    sin_t = jnp.tile(jnp.concatenate([-sin, sin], axis=-1), (1, RET_HEADS))
    j = np.arange(RET_QK_WIDTH)
    partner = np.where((j % RET_DK) < half, j + half, j - half)
    perm = np.zeros((RET_QK_WIDTH, RET_QK_WIDTH), np.float32)
    perm[partner, j] = 1.0
    log_gamma = jnp.log1p(-jnp.exp2(-5.0 - jnp.arange(RET_HEADS, dtype=F32)))
    n = jnp.arange(RET_CHUNK, dtype=F32)
    diff = n[:, None] - n[None, :]
    dec = jnp.where(diff >= 0, jnp.exp(diff[None] * log_gamma[:, None, None]), 0.0)
    zeta = jnp.exp((RET_CHUNK - 1 - n)[:, None] * log_gamma[None, :])
    xi = jnp.exp((n + 1)[:, None] * log_gamma[None, :])
    zeta = jnp.repeat(zeta, RET_DK, axis=1)
    xi = jnp.repeat(xi, RET_DK, axis=1)
    cd = jnp.broadcast_to(jnp.exp(RET_CHUNK * log_gamma)[:, None, None], (RET_HEADS, 128, RET_DV))
    return cos_t, sin_t, jnp.asarray(perm, BF16), dec, zeta, xi, cd


def _scale_row():
    sc = np.ones((1, MAIN_W), np.float32)
    sc[:, 0:A_WIDTH] = HEAD_DIM ** -0.5
    sc[:, A_W + OFF_SQ:A_W + OFF_SQ + SB_WIDTH] = SB_HEAD_DIM ** -0.5 * LOG2E
    sc[:, A_W + OFF_RK:A_W + OFF_RK + RET_QK_WIDTH] = RET_DK ** -0.5
    return jnp.asarray(sc)


def _tri_table():
    k = np.arange(SB_TK)[:, None]
    j = np.arange(SB_TK)[None, :]
    return jnp.asarray(-(k > j).astype(np.float32), BF16)


def kernel(x, rel_bias, norm_mix_w, w_in, b_gate, ret_norm_w, w_proj_a, w_proj_b, w_proj_c, w_out,
           norm_ffn_w, w_up, w_gate, conv_w, conv_b, w_down, final_norm_w):
    b, s, _ = x.shape
    assert b == 1
    xs = x.reshape(s, D_MODEL)
    w_in_bf = w_in.astype(BF16)
    wa, wb, wc, wo = (w.astype(BF16) for w in (w_proj_a, w_proj_b, w_proj_c, w_out))
    wu, wg, wd = (w.astype(BF16) for w in (w_up, w_gate, w_down))
    nmix3 = norm_mix_w.reshape(DEPTH, 1, D_MODEL)
    nffn3 = norm_ffn_w.reshape(DEPTH, 1, D_MODEL)
    rnw3 = ret_norm_w.reshape(DEPTH, 1, RET_V_WIDTH)
    cb3 = conv_b.reshape(DEPTH, 1, D_FF)
    scale_row = _scale_row()
    tri = _tri_table()
    biases = [_dilated_bias(rel_bias, g, w, d) for g, (w, d) in enumerate(DIL_GROUPS)]
    rtabs = _retention_tables(s)

    for l in range(DEPTH):
        a0, a1, a2, pm, gi = _inproj(xs, nmix3, scale_row, w_in_bf, l)
        oas, lses = [], []
        for g, (a_g, (_, dilation)) in enumerate(zip((a0, a1, a2), DIL_GROUPS)):
            o_g, lse_g = _dilated_group(a_g, biases[g], dilation)
            oas.append(o_g)
            lses.append(lse_g)
        yb = _stick_breaking(pm, tri)
        yc = _retention(pm, rtabs, rnw3, l)
        xs = _mix_ffn(xs, oas, lses, yb, yc, gi, b_gate, wa, wb, wc, wo, nffn3, wu, wg, conv_w, cb3, wd,
                      final_norm_w.reshape(1, D_MODEL), l)
    return xs.reshape(b, s, D_MODEL)
```

```python
import functools

import numpy as np
import jax
import jax.numpy as jnp
from jax import lax
from jax.experimental import pallas as pl
from jax.experimental.pallas import tpu as pltpu

F32 = jnp.float32
BF16 = jnp.bfloat16

D_MODEL = 1024
DEPTH = 4
HEAD_DIM = 64
BLOCK = 128
DIL_GROUPS = ((128, 1), (512, 4), (2048, 16))
N_GROUPS = 3
A_HEADS_PER_GROUP = 4
A_WIDTH = 768
A_OUT = 256
SB_HEADS = 4
SB_HEAD_DIM = 128
SB_WIDTH = 512
RET_HEADS = 4
RET_DK = 64
RET_DV = 128
RET_QK_WIDTH = 256
RET_V_WIDTH = 512
RET_CHUNK = 128
ROPE_BASE = 10000.0
REL_BUCKETS = 32
REL_MAX_DIST = 2048
N_BRANCH = 3
D_FF = 2816
EPS = 1e-6

GATE_W = N_BRANCH * D_MODEL
A_W = 3 * A_WIDTH
PM_W = 3 * SB_WIDTH + 2 * RET_QK_WIDTH + 2 * RET_V_WIDTH
MAIN_W = A_W + PM_W
IN_WIDTH = MAIN_W + GATE_W
G_W = 3 * A_OUT
OFF_SQ, OFF_SK, OFF_SV = 0, 512, 1024
OFF_RQ, OFF_RK, OFF_RV, OFF_RG = 1536, 1792, 2048, 2560

NEG = -1e30
VMEM_LIMIT = 56 * 1024 * 1024

NT_DIMS = (((1,), (1,)), ((), ()))
TN_DIMS = (((0,), (0,)), ((), ()))


def _params(sem, vmem=VMEM_LIMIT):
    return pltpu.CompilerParams(dimension_semantics=sem, vmem_limit_bytes=vmem)


def _const_spec(shape, index):
    nd = len(index)
    return pl.BlockSpec(shape, lambda *_: index, pipeline_mode=pl.Buffered(1))


IN_TM = 512
IN_CH = 768
N_SIDE_CAST = 7


def _inproj_body(x_ref, nw_ref, sc_ref, w_ref, *rest):
    nc = N_SIDE_CAST
    cast_in, (a0_ref, a1_ref, a2_ref, pm_ref, gi_ref) = rest[:nc], rest[nc:nc + 5]
    cast_out, ys_ref = rest[nc + 5:2 * nc + 5], rest[2 * nc + 5]
    for src, dst in zip(cast_in, cast_out):
        dst[...] = src[...].astype(BF16)
    x = x_ref[...]
    h = (x * lax.rsqrt(jnp.mean(x * x, axis=-1, keepdims=True) + EPS) * nw_ref[...]).astype(BF16)
    a_refs = (a0_ref, a1_ref, a2_ref)
    for c in range(A_W // IN_CH):
        sl = slice(c * IN_CH, (c + 1) * IN_CH)
        y = jnp.dot(h, w_ref[:, sl], preferred_element_type=F32) * sc_ref[:, sl]
        for j in range(IN_CH // 128):
            ys_ref[j] = y[:, j * 128:(j + 1) * 128]
        for g, (_, d) in enumerate(DIL_GROUPS):
            for r in range(d):
                rows = pl.ds(r, IN_TM // d, stride=d) if d > 1 else slice(None)
                for j in range(A_OUT // 128):
                    col = r * G_W + c * A_OUT + j * 128
                    a_refs[g][:, col:col + 128] = ys_ref[g * (A_OUT // 128) + j, rows, :].astype(BF16)
    for c in range(PM_W // IN_CH):
        sl = slice(A_W + c * IN_CH, A_W + (c + 1) * IN_CH)
        y = jnp.dot(h, w_ref[:, sl], preferred_element_type=F32)
        pm_ref[:, c * IN_CH:(c + 1) * IN_CH] = (y * sc_ref[:, sl]).astype(BF16)
    for c in range(GATE_W // IN_CH):
        y = jnp.dot(h, w_ref[:, MAIN_W + c * IN_CH:MAIN_W + (c + 1) * IN_CH], preferred_element_type=F32)
        gi_ref[:, c * IN_CH:(c + 1) * IN_CH] = y.astype(BF16)


def _cast_specs(w, l, steps):
    _, rows, cols = w.shape
    k = 1
    while (rows * k) % steps or (rows * k // steps) % 16:
        k *= 2
    br = rows * k // steps
    return (pl.BlockSpec((None, br, cols), lambda i: (l, i // k, 0)),
            pl.BlockSpec((br, cols), lambda i: (i // k, 0)),
            jax.ShapeDtypeStruct((rows, cols), BF16))


def _inproj(x, norm_w3, scale_row, w_in_bf, side_weights, l):
    s = x.shape[0]
    steps = s // IN_TM
    a_shapes = [jax.ShapeDtypeStruct((s // d, d * G_W), BF16) for _, d in DIL_GROUPS]
    a_specs = [pl.BlockSpec((IN_TM // d, d * G_W), lambda i: (i, 0)) for _, d in DIL_GROUPS]
    cast = [_cast_specs(w, l, steps) for w in side_weights]
    return pl.pallas_call(
        _inproj_body,
        grid=(steps,),
        in_specs=[
            pl.BlockSpec((IN_TM, D_MODEL), lambda i: (i, 0)),
            _const_spec((None, 1, D_MODEL), (l, 0, 0)),
            _const_spec((1, MAIN_W), (0, 0)),
            _const_spec((D_MODEL, IN_WIDTH), (0, 0)),
        ] + [c[0] for c in cast],
        out_specs=a_specs + [
            pl.BlockSpec((IN_TM, PM_W), lambda i: (i, 0)),
            pl.BlockSpec((IN_TM, GATE_W), lambda i: (i, 0)),
        ] + [c[1] for c in cast],
        out_shape=a_shapes + [jax.ShapeDtypeStruct((s, PM_W), BF16), jax.ShapeDtypeStruct((s, GATE_W), BF16)]
        + [c[2] for c in cast],
        scratch_shapes=[pltpu.VMEM((IN_CH // 128, IN_TM, 128), F32)],
        compiler_params=_params(("arbitrary",)),
        name="inproj",
    )(x, norm_w3, scale_row, w_in_bf, *side_weights)


DIL_NB = 8


def _dil_body(q_ref, k_ref, ke_ref, v_ref, ve_ref, bias_ref, o_ref, lse_ref, *, nblk):
    n = pl.program_id(1)
    lane = lax.broadcasted_iota(jnp.int32, (BLOCK, 128), 1)
    lo = lane < HEAD_DIM
    col = lax.broadcasted_iota(jnp.int32, (BLOCK, 2 * BLOCK), 1)
    first_ok = jnp.logical_or(n > 0, col >= BLOCK)
    for b in range(nblk):
        for p in range(2):
            sl = slice(p * 128, (p + 1) * 128)
            q = q_ref[b * BLOCK:(b + 1) * BLOCK, sl]
            if b == 0:
                k2 = jnp.concatenate([ke_ref[:, sl], k_ref[0:BLOCK, sl]], axis=0)
                v2 = jnp.concatenate([ve_ref[:, sl], v_ref[0:BLOCK, sl]], axis=0)
            else:
                k2 = k_ref[(b - 1) * BLOCK:(b + 1) * BLOCK, sl]
                v2 = v_ref[(b - 1) * BLOCK:(b + 1) * BLOCK, sl]
            outs, lses = [], []
            for hh in range(2):
                h = 2 * p + hh
                sel = lo if hh == 0 else jnp.logical_not(lo)
                qm = jnp.where(sel, q, jnp.zeros_like(q))
                s = lax.dot_general(qm, k2, NT_DIMS, preferred_element_type=F32) + bias_ref[h]
                if b == 0:
                    s = jnp.where(first_ok, s, NEG)
                m = jnp.max(s, axis=-1, keepdims=True)
                e = jnp.exp(s - m)
                den = jnp.sum(e, axis=-1, keepdims=True)
                num = jnp.dot(e.astype(BF16), v2, preferred_element_type=F32)
                outs.append(num / den)
                lses.append(jnp.broadcast_to(m + jnp.log(den), (BLOCK, 128)))
            o_ref[b * BLOCK:(b + 1) * BLOCK, sl] = jnp.where(lo, outs[0], outs[1])
            lse_ref[b * BLOCK:(b + 1) * BLOCK, sl] = jnp.where(lo, lses[0], lses[1])


def _dilated_group(a_g, bias_g, dilation):
    rows = a_g.shape[0]
    nblk = min(DIL_NB, rows // BLOCK)
    tr = nblk * BLOCK
    cb = G_W // A_OUT

    def cur(off):
        return pl.BlockSpec((tr, A_OUT), lambda r, n: (n, r * cb + off))

    def edge(off):
        return pl.BlockSpec((BLOCK, A_OUT), lambda r, n: (jnp.maximum(n * nblk - 1, 0), r * cb + off))

    return pl.pallas_call(
        functools.partial(_dil_body, nblk=nblk),
        grid=(dilation, rows // tr),
        in_specs=[cur(0), cur(1), edge(1), cur(2), edge(2),
                  _const_spec((A_HEADS_PER_GROUP, BLOCK, 2 * BLOCK), (0, 0, 0))],
        out_specs=[pl.BlockSpec((tr, A_OUT), lambda r, n: (n, r)),
                   pl.BlockSpec((tr, A_OUT), lambda r, n: (n, r))],
        out_shape=[jax.ShapeDtypeStruct((rows, dilation * A_OUT), F32),
                   jax.ShapeDtypeStruct((rows, dilation * A_OUT), F32)],
        compiler_params=_params(("parallel", "parallel")),
        name=f"dilated_attn_d{dilation}",
    )(a_g, a_g, a_g, a_g, a_g, bias_g)


SB_TQ = 1024
SB_TK = 256
SB_ND = SB_TQ // SB_TK
LOG2E = 1.4426950408889634
SB_SOFTPLUS_LINEAR = 64.0
SB_UNDERFLOW_LOG2 = -160.0


def _sb_body(q_ref, k_ref, v_ref, tri_ref, o_ref, acc_ref, c_ref, z_ref, w_ref, live_ref):
    i = pl.program_id(1)
    qi = i * SB_TQ
    acc_ref[...] = jnp.zeros_like(acc_ref)
    c_ref[...] = jnp.zeros_like(c_ref)

    def key_start(t):
        return pl.multiple_of(qi + (SB_ND - 1 - t) * SB_TK, SB_TK)

    def first_row(t):
        return max(0, SB_ND - 1 - t) * SB_TK if isinstance(t, int) else 0

    def p1(t, slot):
        r0 = first_row(t)
        kb = k_ref[pl.ds(key_start(t), SB_TK), :]
        z_ref[slot, r0:, :] = lax.dot_general(q_ref[r0:, :], kb, NT_DIMS, preferred_element_type=F32)

    def p23(t, slot, masked):
        r0 = first_row(t)
        z = z_ref[slot, r0:, :]
        sp = jnp.where(z > SB_SOFTPLUS_LINEAR, z, jnp.log2(1.0 + jnp.exp2(z)))
        c = c_ref[r0:, :]
        u0 = z - sp + jnp.concatenate([c, c], axis=1)
        if masked:
            shape = (SB_TQ - r0, SB_TK)
            row = lax.broadcasted_iota(jnp.int32, shape, 0) + r0
            col = lax.broadcasted_iota(jnp.int32, shape, 1)
            mask = (col + (SB_ND - 1 - t) * SB_TK) < row
            sp = jnp.where(mask, sp, 0.0)
            u0 = jnp.where(mask, u0, NEG)
        c_ref[r0:, :] = c - jnp.sum(sp, axis=-1, keepdims=True)
        live_ref[0] = (jnp.max(c_ref[r0:, :]) > SB_UNDERFLOW_LOG2).astype(jnp.int32)
        u = u0 + jnp.dot(sp.astype(BF16), tri_ref[...], preferred_element_type=F32)
        w_ref[slot, r0:, :] = jnp.exp2(u.astype(BF16))

    def p4(t, slot):
        r0 = first_row(t)
        vb = v_ref[pl.ds(key_start(t), SB_TK), :]
        acc_ref[r0:, :] += jnp.dot(w_ref[slot, r0:, :], vb, preferred_element_type=F32)

    n = (i + 1) * SB_ND

    def trip(it, slot, masked=False, stages=(True, True, True)):
        if stages[2]:
            p4(it - 2, slot)
        if stages[1]:
            p23(it - 1, 1 - slot, masked)
        if stages[0]:
            p1(it, slot)

    @pl.when(i == 0)
    def _():
        for t in range(SB_ND):
            p1(t, t % 2)
            p23(t, t % 2, True)
            p4(t, t % 2)

    @pl.when(i > 0)
    def _():
        for it in range(SB_ND + 1):
            trip(it, it % 2, True, (True, it >= 1, it >= 2))

        def cond(carry):
            it, live = carry
            return jnp.logical_and(it < n, live > 0)

        def body(carry):
            it, _ = carry
            for slot in range(2):
                @pl.when(it % 2 == slot)
                def _():
                    trip(it, slot)
            return it + 1, live_ref[0]

        it_end, live = lax.while_loop(cond, body, (jnp.int32(SB_ND + 1), live_ref[0]))
        for slot in range(2):
            @pl.when(it_end % 2 == slot)
            def _():
                p4(it_end - 2, slot)

            @pl.when(jnp.logical_and(it_end % 2 == slot, live > 0))
            def _():
                p23(it_end - 1, 1 - slot, False)
                p4(it_end - 1, 1 - slot)

    o_ref[...] = acc_ref[...].astype(o_ref.dtype)


def _stick_breaking(pm, tri):
    s = pm.shape[0]
    qb, kb, vb = OFF_SQ // 128, OFF_SK // 128, OFF_SV // 128
    return pl.pallas_call(
        _sb_body,
        grid=(SB_HEADS, s // SB_TQ),
        in_specs=[
            pl.BlockSpec((SB_TQ, 128), lambda h, i: (i, qb + h)),
            pl.BlockSpec((s, 128), lambda h, i: (0, kb + h)),
            pl.BlockSpec((s, 128), lambda h, i: (0, vb + h)),
            _const_spec((SB_TK, SB_TK), (0, 0)),
        ],
        out_specs=pl.BlockSpec((SB_TQ, 128), lambda h, i: (i, h)),
        out_shape=jax.ShapeDtypeStruct((s, SB_WIDTH), BF16),
        scratch_shapes=[pltpu.VMEM((SB_TQ, 128), F32), pltpu.VMEM((SB_TQ, 128), F32),
                        pltpu.VMEM((2, SB_TQ, SB_TK), F32), pltpu.VMEM((2, SB_TQ, SB_TK), BF16),
                        pltpu.SMEM((1,), jnp.int32)],
        compiler_params=_params(("parallel", "parallel")),
        name="stick_breaking",
    )(pm, pm, pm, tri)


RET_NC = 8


def _ret_body(q_ref, k_ref, v_ref, g_ref, cos_ref, sin_ref, perm_ref, dec_ref, zeta_ref, xi_ref,
              cd_ref, nw_ref, o_ref, r_ref, *, nc):
    n = pl.program_id(0)

    @pl.when(n == 0)
    def _():
        r_ref[...] = jnp.zeros_like(r_ref)

    cos = cos_ref[...]
    sin = sin_ref[...]
    perm = perm_ref[...]

    def rot(x):
        swapped = jnp.dot(x, perm, preferred_element_type=F32)
        return x.astype(F32) * cos + swapped * sin

    rq = rot(q_ref[...])
    rk = rot(k_ref[...])
    qb = rq.astype(BF16)
    qx = (rq * xi_ref[...]).astype(BF16)
    kb = rk.astype(BF16)
    kz = (rk * zeta_ref[...]).astype(BF16)
    lane = lax.broadcasted_iota(jnp.int32, (RET_CHUNK, 128), 1)
    lo = lane < RET_DK
    zero = jnp.zeros((RET_CHUNK, 128), BF16)
    for h in range(RET_HEADS):
        sl = slice((h // 2) * 128, (h // 2 + 1) * 128)
        sel = lo if h % 2 == 0 else jnp.logical_not(lo)
        hs = slice(h * RET_DV, (h + 1) * RET_DV)
        r = r_ref[h]
        for c in range(nc):
            rs = slice(c * RET_CHUNK, (c + 1) * RET_CHUNK)
            v = v_ref[rs, hs]
            qm = jnp.where(sel, qb[rs, sl], zero)
            qxm = jnp.where(sel, qx[rs, sl], zero)
            intra = lax.dot_general(qm, kb[rs, sl], NT_DIMS, preferred_element_type=F32) * dec_ref[h]
            o = jnp.dot(intra.astype(BF16), v, preferred_element_type=F32)
            o = o + jnp.dot(qxm, r.astype(BF16), preferred_element_type=F32)
            kv = lax.dot_general(kz[rs, sl], v, TN_DIMS, preferred_element_type=F32)
            r = r * cd_ref[h] + kv
            mu = jnp.mean(o, axis=-1, keepdims=True)
            d = o - mu
            var = jnp.mean(d * d, axis=-1, keepdims=True)
            on = (d * lax.rsqrt(var + EPS)) * nw_ref[:, hs]
            gate = g_ref[rs, hs].astype(F32)
            o_ref[rs, hs] = (gate * jax.nn.sigmoid(gate) * on).astype(o_ref.dtype)
        r_ref[h] = r


def _retention(pm, tabs, ret_nw3, l):
    s = pm.shape[0]
    cos_t, sin_t, perm, dec, zeta, xi, cd = tabs
    nc = min(RET_NC, s // RET_CHUNK)
    rows = nc * RET_CHUNK
    zeta_t, xi_t = jnp.tile(zeta, (nc, 1)), jnp.tile(xi, (nc, 1))
    return pl.pallas_call(
        functools.partial(_ret_body, nc=nc),
        grid=(s // rows,),
        in_specs=[
            pl.BlockSpec((rows, 256), lambda n: (n, OFF_RQ // 256)),
            pl.BlockSpec((rows, 256), lambda n: (n, OFF_RK // 256)),
            pl.BlockSpec((rows, 512), lambda n: (n, OFF_RV // 512)),
            pl.BlockSpec((rows, 512), lambda n: (n, OFF_RG // 512)),
            pl.BlockSpec((rows, 256), lambda n: (n, 0)),
            pl.BlockSpec((rows, 256), lambda n: (n, 0)),
            _const_spec((256, 256), (0, 0)),
            _const_spec((RET_HEADS, RET_CHUNK, RET_CHUNK), (0, 0, 0)),
            _const_spec((rows, 256), (0, 0)),
            _const_spec((rows, 256), (0, 0)),
            _const_spec((RET_HEADS, 128, RET_DV), (0, 0, 0)),
            _const_spec((None, 1, RET_V_WIDTH), (l, 0, 0)),
        ],
        out_specs=pl.BlockSpec((rows, RET_V_WIDTH), lambda n: (n, 0)),
        out_shape=jax.ShapeDtypeStruct((s, RET_V_WIDTH), BF16),
        scratch_shapes=[pltpu.VMEM((RET_HEADS, 128, RET_DV), F32)],
        compiler_params=_params(("arbitrary",)),
        name="retention",
    )(pm, pm, pm, pm, cos_t, sin_t, perm, dec, zeta_t, xi_t, cd, ret_nw3)


MF_TM = 512
FF_CH = 256
CARRY = 8
MF_VMEM_LIMIT = 60 * 1024 * 1024


def _gelu_tanh(x):
    return 0.5 * x * (1.0 + jnp.tanh(np.sqrt(2.0 / np.pi).astype(np.float32) * (x + 0.044715 * (x * x * x))))


def _rms(x, w):
    return x * lax.rsqrt(jnp.mean(x * x, axis=-1, keepdims=True) + EPS) * w


def _mix_ffn_body(x_ref, o0, o1, o2, l0, l1, l2, yb_ref, yc_ref, gi_ref, bg_ref, wa_ref, wb_ref, wc_ref, wo_ref,
                  nw_ref, wu_ref, wg_ref, cw_ref, cb_ref, wd_ref, fw_ref, *rest, final):
    if final:
        out_ref, so1, so2, sl1, sl2, u_ref, carry_ref, a_ref = rest
    else:
        w_next_ref, out_ref, w_next_bf_ref, so1, so2, sl1, sl2, u_ref, carry_ref, a_ref = rest
        w_next_bf_ref[...] = w_next_ref[...].astype(BF16)
    i = pl.program_id(0)

    @pl.when(i == 0)
    def _():
        carry_ref[...] = jnp.zeros_like(carry_ref)

    for src, dst, d in ((o1, so1, DIL_GROUPS[1][1]), (l1, sl1, DIL_GROUPS[1][1]),
                        (o2, so2, DIL_GROUPS[2][1]), (l2, sl2, DIL_GROUPS[2][1])):
        for r in range(d):
            for j in range(A_OUT // 128):
                col = r * A_OUT + j * 128
                dst[j, pl.ds(r, MF_TM // d, stride=d), :] = src[:, col:col + 128]

    def tok(ref):
        return jnp.concatenate([ref[j] for j in range(A_OUT // 128)], axis=1)

    a0, a1, a2 = l0[...], tok(sl1), tok(sl2)
    m = jnp.maximum(jnp.maximum(a0, a1), a2)
    e0, e1, e2 = jnp.exp(a0 - m), jnp.exp(a1 - m), jnp.exp(a2 - m)
    ya = (e0 * o0[...] + e1 * tok(so1) + e2 * tok(so2)) / (e0 + e1 + e2)
    pa = jnp.dot(ya.astype(BF16), wa_ref[...], preferred_element_type=F32)
    pb = jnp.dot(yb_ref[...], wb_ref[...], preferred_element_type=F32)
    pc = jnp.dot(yc_ref[...], wc_ref[...], preferred_element_type=F32)
    merged = None
    for b, pr in enumerate((pa, pb, pc)):
        gate = jax.nn.sigmoid(gi_ref[:, b * D_MODEL:(b + 1) * D_MODEL].astype(F32) + bg_ref[b:b + 1, :])
        merged = gate * pr if merged is None else merged + gate * pr
    x = x_ref[...] + jnp.dot(merged.astype(BF16), wo_ref[...], preferred_element_type=F32)

    h = _rms(x, nw_ref[...]).astype(BF16)
    for c in range(D_FF // FF_CH):
        sl = slice(c * FF_CH, (c + 1) * FF_CH)
        ub = u_ref.at[c % 2]
        ub[0:CARRY, :] = carry_ref[:, sl]
        ub[CARRY:CARRY + MF_TM, :] = jnp.dot(h, wu_ref[:, sl], preferred_element_type=F32)
        u0 = ub[CARRY:CARRY + MF_TM, :]
        u1 = ub[CARRY - 1:CARRY - 1 + MF_TM, :]
        u2 = ub[CARRY - 2:CARRY - 2 + MF_TM, :]
        carry_ref[:, sl] = ub[MF_TM:MF_TM + CARRY, :]
        uc = cw_ref[0:1, sl] * u2 + cw_ref[1:2, sl] * u1 + cw_ref[2:3, sl] * u0 + cb_ref[:, sl]
        g = jnp.dot(h, wg_ref[:, sl], preferred_element_type=F32)
        a_ref[:, sl] = (_gelu_tanh(uc) * g).astype(BF16)
    y = x + jnp.dot(a_ref[...], wd_ref[...], preferred_element_type=F32)
    out_ref[...] = _rms(y, fw_ref[...]) if final else y


def _mix_ffn(x, oas, lses, yb, yc, gi, b_gate, wa, wb, wc, wo, nw3, wu, wg, cw, cb3, wd, fw, w_in, l):
    s = x.shape[0]
    tm = MF_TM
    final = l == DEPTH - 1
    row = lambda w: pl.BlockSpec((tm, w), lambda i: (i, 0))
    grp = [pl.BlockSpec((tm // d, d * A_OUT), lambda i: (i, 0)) for _, d in DIL_GROUPS]
    cast = None if final else _cast_specs(w_in, l + 1, s // tm)
    return pl.pallas_call(
        functools.partial(_mix_ffn_body, final=final),
        grid=(s // tm,),
        in_specs=[row(D_MODEL)] + grp + grp + [row(SB_WIDTH), row(RET_V_WIDTH), row(GATE_W),
                  _const_spec((None, N_BRANCH, D_MODEL), (l, 0, 0)),
                  _const_spec((A_OUT, D_MODEL), (0, 0)),
                  _const_spec((SB_WIDTH, D_MODEL), (0, 0)),
                  _const_spec((RET_V_WIDTH, D_MODEL), (0, 0)),
                  _const_spec((D_MODEL, D_MODEL), (0, 0)),
                  _const_spec((None, 1, D_MODEL), (l, 0, 0)),
                  _const_spec((D_MODEL, D_FF), (0, 0)),
                  _const_spec((D_MODEL, D_FF), (0, 0)),
                  _const_spec((None, 3, D_FF), (l, 0, 0)),
                  _const_spec((None, 1, D_FF), (l, 0, 0)),
                  _const_spec((D_FF, D_MODEL), (0, 0)),
                  _const_spec((1, D_MODEL), (0, 0))] + ([] if final else [cast[0]]),
        out_specs=[row(D_MODEL)] + ([] if final else [cast[1]]),
        out_shape=[jax.ShapeDtypeStruct((s, D_MODEL), F32)] + ([] if final else [cast[2]]),
        scratch_shapes=[pltpu.VMEM((A_OUT // 128, tm, 128), F32)] * 4 + [
            pltpu.VMEM((2, tm + CARRY, FF_CH), F32), pltpu.VMEM((CARRY, D_FF), F32),
            pltpu.VMEM((tm, D_FF), BF16)],
        compiler_params=_params(("arbitrary",), MF_VMEM_LIMIT),
        name="mix_ffn",
    )(x, *oas, *lses, yb, yc, gi, b_gate, wa, wb, wc, wo, nw3, wu, wg, cw, cb3, wd, fw, *([] if final else [w_in]))


def _t5_causal_bucket(dist):
    max_exact = REL_BUCKETS // 2
    n = np.maximum(dist, 0)
    large = max_exact + (np.log(np.maximum(n, 1) / max_exact) / np.log(REL_MAX_DIST / max_exact)
                         * (REL_BUCKETS - max_exact)).astype(np.int64)
    large = np.minimum(large, REL_BUCKETS - 1)
    return np.where(n < max_exact, n, large).astype(np.int32)


def _dilated_bias(rel_bias, g, window, dilation):
    n_steps = window // dilation
    steps = np.arange(BLOCK)[:, None] + BLOCK - np.arange(2 * BLOCK)[None, :]
    in_band = (steps >= 0) & (steps <= n_steps)
    bucket = _t5_causal_bucket(steps * dilation)
    bias_g = rel_bias[:, g * A_HEADS_PER_GROUP:(g + 1) * A_HEADS_PER_GROUP].astype(F32)
    onehot = (jnp.asarray(bucket)[:, :, None] == jnp.arange(REL_BUCKETS)[None, None, :]).astype(F32)
    bias = jnp.einsum('qkb,bh->hqk', onehot, bias_g, precision=lax.Precision.HIGHEST)
    return jnp.where(jnp.asarray(in_band)[None], bias, NEG)


def _retention_tables(s):
    half = RET_DK // 2
    inv = ROPE_BASE ** (-jnp.arange(half, dtype=F32) / half)
    ang = jnp.arange(s, dtype=jnp.int32).astype(F32)[:, None] * inv[None, :]
    cos, sin = jnp.cos(ang), jnp.sin(ang)
    cos_t = jnp.tile(jnp.concatenate([cos, cos], axis=-1), (1, RET_HEADS))
    sin_t = jnp.tile(jnp.concatenate([-sin, sin], axis=-1), (1, RET_HEADS))
    j = np.arange(RET_QK_WIDTH)
    partner = np.where((j % RET_DK) < half, j + half, j - half)
    perm = np.zeros((RET_QK_WIDTH, RET_QK_WIDTH), np.float32)
    perm[partner, j] = 1.0
    log_gamma = jnp.log1p(-jnp.exp2(-5.0 - jnp.arange(RET_HEADS, dtype=F32)))
    n = jnp.arange(RET_CHUNK, dtype=F32)
    diff = n[:, None] - n[None, :]
    dec = jnp.where(diff >= 0, jnp.exp(diff[None] * log_gamma[:, None, None]), 0.0)
    zeta = jnp.exp((RET_CHUNK - 1 - n)[:, None] * log_gamma[None, :])
    xi = jnp.exp((n + 1)[:, None] * log_gamma[None, :])
    zeta = jnp.repeat(zeta, RET_DK, axis=1)
    xi = jnp.repeat(xi, RET_DK, axis=1)
    cd = jnp.broadcast_to(jnp.exp(RET_CHUNK * log_gamma)[:, None, None], (RET_HEADS, 128, RET_DV))
    return cos_t, sin_t, jnp.asarray(perm, BF16), dec, zeta, xi, cd


def _scale_row():
    sc = np.ones((1, MAIN_W), np.float32)
    sc[:, 0:A_WIDTH] = HEAD_DIM ** -0.5
    sc[:, A_W + OFF_SQ:A_W + OFF_SQ + SB_WIDTH] = SB_HEAD_DIM ** -0.5 * LOG2E
    sc[:, A_W + OFF_RK:A_W + OFF_RK + RET_QK_WIDTH] = RET_DK ** -0.5
    return jnp.asarray(sc)


def _tri_table():
    k = np.arange(SB_TK)[:, None]
    j = np.arange(SB_TK)[None, :]
    return jnp.asarray(-(k > j).astype(np.float32), BF16)


def kernel(x, rel_bias, norm_mix_w, w_in, b_gate, ret_norm_w, w_proj_a, w_proj_b, w_proj_c, w_out,
           norm_ffn_w, w_up, w_gate, conv_w, conv_b, w_down, final_norm_w):
    b, s, _ = x.shape
    assert b == 1
    xs = x.reshape(s, D_MODEL)
    w_in_bf = w_in[0].astype(BF16)
    side_weights = (w_proj_a, w_proj_b, w_proj_c, w_out, w_up, w_gate, w_down)
    assert len(side_weights) == N_SIDE_CAST
    nmix3 = norm_mix_w.reshape(DEPTH, 1, D_MODEL)
    nffn3 = norm_ffn_w.reshape(DEPTH, 1, D_MODEL)
    rnw3 = ret_norm_w.reshape(DEPTH, 1, RET_V_WIDTH)
    cb3 = conv_b.reshape(DEPTH, 1, D_FF)
    scale_row = _scale_row()
    tri = _tri_table()
    biases = [_dilated_bias(rel_bias, g, w, d) for g, (w, d) in enumerate(DIL_GROUPS)]
    rtabs = _retention_tables(s)

    for l in range(DEPTH):
        a0, a1, a2, pm, gi, wa, wb, wc, wo, wu, wg, wd = _inproj(xs, nmix3, scale_row, w_in_bf, side_weights, l)
        oas, lses = [], []
        for g, (a_g, (_, dilation)) in enumerate(zip((a0, a1, a2), DIL_GROUPS)):
            o_g, lse_g = _dilated_group(a_g, biases[g], dilation)
            oas.append(o_g)
            lses.append(lse_g)
        yb = _stick_breaking(pm, tri)
        yc = _retention(pm, rtabs, rnw3, l)
        res = _mix_ffn(xs, oas, lses, yb, yc, gi, b_gate, wa, wb, wc, wo, nffn3, wu, wg, conv_w, cb3, wd,
                       final_norm_w.reshape(1, D_MODEL), w_in, l)
        xs = res[0]
        if l < DEPTH - 1:
            w_in_bf = res[1]
    return xs.reshape(b, s, D_MODEL)
```

```python
import functools

import numpy as np
import jax
import jax.numpy as jnp
from jax import lax
from jax.experimental import pallas as pl
from jax.experimental.pallas import tpu as pltpu

F32 = jnp.float32
BF16 = jnp.bfloat16

D_MODEL = 1024
DEPTH = 4
HEAD_DIM = 64
BLOCK = 128
DIL_GROUPS = ((128, 1), (512, 4), (2048, 16))
N_GROUPS = 3
A_HEADS_PER_GROUP = 4
A_WIDTH = 768
A_OUT = 256
SB_HEADS = 4
SB_HEAD_DIM = 128
SB_WIDTH = 512
RET_HEADS = 4
RET_DK = 64
RET_DV = 128
RET_QK_WIDTH = 256
RET_V_WIDTH = 512
RET_CHUNK = 128
ROPE_BASE = 10000.0
REL_BUCKETS = 32
REL_MAX_DIST = 2048
N_BRANCH = 3
D_FF = 2816
EPS = 1e-6

GATE_W = N_BRANCH * D_MODEL
A_W = 3 * A_WIDTH
PM_W = 3 * SB_WIDTH + 2 * RET_QK_WIDTH + 2 * RET_V_WIDTH
MAIN_W = A_W + PM_W
IN_WIDTH = MAIN_W + GATE_W
G_W = 3 * A_OUT
OFF_SQ, OFF_SK, OFF_SV = 0, 512, 1024
OFF_RQ, OFF_RK, OFF_RV, OFF_RG = 1536, 1792, 2048, 2560

NEG = -1e30
VMEM_LIMIT = 56 * 1024 * 1024

NT_DIMS = (((1,), (1,)), ((), ()))
TN_DIMS = (((0,), (0,)), ((), ()))


def _params(sem, vmem=VMEM_LIMIT):
    return pltpu.CompilerParams(dimension_semantics=sem, vmem_limit_bytes=vmem)


def _const_spec(shape, index):
    nd = len(index)
    return pl.BlockSpec(shape, lambda *_: index, pipeline_mode=pl.Buffered(1))


IN_TM = 512
IN_CH = 768
N_SIDE_CAST = 7


def _inproj_body(x_ref, nw_ref, sc_ref, w_ref, *rest):
    nc = N_SIDE_CAST
    cast_in, (a0_ref, a1_ref, a2_ref, pm_ref, gi_ref) = rest[:nc], rest[nc:nc + 5]
    cast_out, ys_ref = rest[nc + 5:2 * nc + 5], rest[2 * nc + 5]
    for src, dst in zip(cast_in, cast_out):
        dst[...] = src[...].astype(BF16)
    x = x_ref[...]
    h = (x * lax.rsqrt(jnp.mean(x * x, axis=-1, keepdims=True) + EPS) * nw_ref[...]).astype(BF16)
    a_refs = (a0_ref, a1_ref, a2_ref)
    for c in range(A_W // IN_CH):
        sl = slice(c * IN_CH, (c + 1) * IN_CH)
        y = jnp.dot(h, w_ref[:, sl], preferred_element_type=F32) * sc_ref[:, sl]
        for j in range(IN_CH // 128):
            ys_ref[j] = y[:, j * 128:(j + 1) * 128]
        for g, (_, d) in enumerate(DIL_GROUPS):
            for r in range(d):
                rows = pl.ds(r, IN_TM // d, stride=d) if d > 1 else slice(None)
                for j in range(A_OUT // 128):
                    col = r * G_W + c * A_OUT + j * 128
                    a_refs[g][:, col:col + 128] = ys_ref[g * (A_OUT // 128) + j, rows, :].astype(BF16)
    for c in range(PM_W // IN_CH):
        sl = slice(A_W + c * IN_CH, A_W + (c + 1) * IN_CH)
        y = jnp.dot(h, w_ref[:, sl], preferred_element_type=F32)
        pm_ref[:, c * IN_CH:(c + 1) * IN_CH] = (y * sc_ref[:, sl]).astype(BF16)
    for c in range(GATE_W // IN_CH):
        y = jnp.dot(h, w_ref[:, MAIN_W + c * IN_CH:MAIN_W + (c + 1) * IN_CH], preferred_element_type=F32)
        gi_ref[:, c * IN_CH:(c + 1) * IN_CH] = y.astype(BF16)


def _cast_specs(w, l, steps):
    _, rows, cols = w.shape
    k = 1
    while (rows * k) % steps or (rows * k // steps) % 16:
        k *= 2
    br = rows * k // steps
    return (pl.BlockSpec((None, br, cols), lambda i: (l, i // k, 0)),
            pl.BlockSpec((br, cols), lambda i: (i // k, 0)),
            jax.ShapeDtypeStruct((rows, cols), BF16))


def _inproj(x, norm_w3, scale_row, w_in_bf, side_weights, l):
    s = x.shape[0]
    steps = s // IN_TM
    a_shapes = [jax.ShapeDtypeStruct((s // d, d * G_W), BF16) for _, d in DIL_GROUPS]
    a_specs = [pl.BlockSpec((IN_TM // d, d * G_W), lambda i: (i, 0)) for _, d in DIL_GROUPS]
    cast = [_cast_specs(w, l, steps) for w in side_weights]
    return pl.pallas_call(
        _inproj_body,
        grid=(steps,),
        in_specs=[
            pl.BlockSpec((IN_TM, D_MODEL), lambda i: (i, 0)),
            _const_spec((None, 1, D_MODEL), (l, 0, 0)),
            _const_spec((1, MAIN_W), (0, 0)),
            _const_spec((D_MODEL, IN_WIDTH), (0, 0)),
        ] + [c[0] for c in cast],
        out_specs=a_specs + [
            pl.BlockSpec((IN_TM, PM_W), lambda i: (i, 0)),
            pl.BlockSpec((IN_TM, GATE_W), lambda i: (i, 0)),
        ] + [c[1] for c in cast],
        out_shape=a_shapes + [jax.ShapeDtypeStruct((s, PM_W), BF16), jax.ShapeDtypeStruct((s, GATE_W), BF16)]
        + [c[2] for c in cast],
        scratch_shapes=[pltpu.VMEM((IN_CH // 128, IN_TM, 128), F32)],
        compiler_params=_params(("arbitrary",)),
        name="inproj",
    )(x, norm_w3, scale_row, w_in_bf, *side_weights)


DIL_STEPS = max(d for _, d in DIL_GROUPS)


def _dil_group(n, q_ref, k_ref, ke_ref, v_ref, ve_ref, bias_ref, o_ref, lse_ref, nblk):
    lane = lax.broadcasted_iota(jnp.int32, (BLOCK, 128), 1)
    lo = lane < HEAD_DIM
    col = lax.broadcasted_iota(jnp.int32, (BLOCK, 2 * BLOCK), 1)
    first_ok = jnp.logical_or(n > 0, col >= BLOCK)
    for b in range(nblk):
        for p in range(2):
            sl = slice(p * 128, (p + 1) * 128)
            q = q_ref[b * BLOCK:(b + 1) * BLOCK, sl]
            if b == 0:
                k2 = jnp.concatenate([ke_ref[:, sl], k_ref[0:BLOCK, sl]], axis=0)
                v2 = jnp.concatenate([ve_ref[:, sl], v_ref[0:BLOCK, sl]], axis=0)
            else:
                k2 = k_ref[(b - 1) * BLOCK:(b + 1) * BLOCK, sl]
                v2 = v_ref[(b - 1) * BLOCK:(b + 1) * BLOCK, sl]
            outs, lses = [], []
            for hh in range(2):
                h = 2 * p + hh
                sel = lo if hh == 0 else jnp.logical_not(lo)
                qm = jnp.where(sel, q, jnp.zeros_like(q))
                s = lax.dot_general(qm, k2, NT_DIMS, preferred_element_type=F32) + bias_ref[h]
                if b == 0:
                    s = jnp.where(first_ok, s, NEG)
                m = jnp.max(s, axis=-1, keepdims=True)
                e = jnp.exp(s - m)
                den = jnp.sum(e, axis=-1, keepdims=True)
                num = jnp.dot(e.astype(BF16), v2, preferred_element_type=F32)
                outs.append(num / den)
                lses.append(jnp.broadcast_to(m + jnp.log(den), (BLOCK, 128)))
            o_ref[b * BLOCK:(b + 1) * BLOCK, sl] = jnp.where(lo, outs[0], outs[1])
            lse_ref[b * BLOCK:(b + 1) * BLOCK, sl] = jnp.where(lo, lses[0], lses[1])


def _dil_body(*refs, nblk):
    j = pl.program_id(0)
    ng = len(DIL_GROUPS)
    for g, (_, d) in enumerate(DIL_GROUPS):
        n = j % (DIL_STEPS // d)
        _dil_group(n, *refs[6 * g:6 * g + 6], *refs[6 * ng + 2 * g:6 * ng + 2 * g + 2], nblk)


def _dilated_attention(a_groups, biases):
    s = a_groups[0].shape[0] * DIL_GROUPS[0][1]
    tr = s // DIL_STEPS
    nblk = tr // BLOCK
    cb = G_W // A_OUT
    in_specs, out_specs, out_shape = [], [], []
    for (_, d) in DIL_GROUPS:
        per = DIL_STEPS // d

        def cur(off, per=per):
            return pl.BlockSpec((tr, A_OUT), lambda j: (j % per, (j // per) * cb + off))

        def edge(off, per=per):
            return pl.BlockSpec((BLOCK, A_OUT),
                                lambda j: (jnp.maximum((j % per) * nblk - 1, 0), (j // per) * cb + off))

        in_specs += [cur(0), cur(1), edge(1), cur(2), edge(2),
                     _const_spec((A_HEADS_PER_GROUP, BLOCK, 2 * BLOCK), (0, 0, 0))]
        out_specs += [pl.BlockSpec((tr, A_OUT), lambda j, per=per: (j % per, j // per))] * 2
        out_shape += [jax.ShapeDtypeStruct((s // d, d * A_OUT), F32)] * 2
    args = []
    for a_g, bias_g in zip(a_groups, biases):
        args += [a_g] * 5 + [bias_g]
    outs = pl.pallas_call(
        functools.partial(_dil_body, nblk=nblk),
        grid=(DIL_STEPS,),
        in_specs=in_specs,
        out_specs=out_specs,
        out_shape=out_shape,
        compiler_params=_params(("parallel",)),
        name="dilated_attn",
    )(*args)
    return outs[0::2], outs[1::2]


SB_TQ = 1024
SB_TK = 256
SB_ND = SB_TQ // SB_TK
LOG2E = 1.4426950408889634
SB_SOFTPLUS_LINEAR = 64.0
SB_UNDERFLOW_LOG2 = -160.0


def _sb_body(q_ref, k_ref, v_ref, tri_ref, o_ref, acc_ref, c_ref, z_ref, w_ref, live_ref):
    i = pl.program_id(1)
    qi = i * SB_TQ
    acc_ref[...] = jnp.zeros_like(acc_ref)
    c_ref[...] = jnp.zeros_like(c_ref)

    def key_start(t):
        return pl.multiple_of(qi + (SB_ND - 1 - t) * SB_TK, SB_TK)

    def first_row(t):
        return max(0, SB_ND - 1 - t) * SB_TK if isinstance(t, int) else 0

    def p1(t, slot):
        r0 = first_row(t)
        kb = k_ref[pl.ds(key_start(t), SB_TK), :]
        z_ref[slot, r0:, :] = lax.dot_general(q_ref[r0:, :], kb, NT_DIMS, preferred_element_type=F32)

    def p23(t, slot, masked):
        r0 = first_row(t)
        z = z_ref[slot, r0:, :]
        sp = jnp.where(z > SB_SOFTPLUS_LINEAR, z, jnp.log2(1.0 + jnp.exp2(z)))
        c = c_ref[r0:, :]
        u0 = z - sp + jnp.concatenate([c, c], axis=1)
        if masked:
            shape = (SB_TQ - r0, SB_TK)
            row = lax.broadcasted_iota(jnp.int32, shape, 0) + r0
            col = lax.broadcasted_iota(jnp.int32, shape, 1)
            mask = (col + (SB_ND - 1 - t) * SB_TK) < row
            sp = jnp.where(mask, sp, 0.0)
            u0 = jnp.where(mask, u0, NEG)
        c_ref[r0:, :] = c - jnp.sum(sp, axis=-1, keepdims=True)
        live_ref[0] = (jnp.max(c_ref[r0:, :]) > SB_UNDERFLOW_LOG2).astype(jnp.int32)
        u = u0 + jnp.dot(sp.astype(BF16), tri_ref[...], preferred_element_type=F32)
        w_ref[slot, r0:, :] = jnp.exp2(u.astype(BF16))

    def p4(t, slot):
        r0 = first_row(t)
        vb = v_ref[pl.ds(key_start(t), SB_TK), :]
        acc_ref[r0:, :] += jnp.dot(w_ref[slot, r0:, :], vb, preferred_element_type=F32)

    n = (i + 1) * SB_ND

    def trip(it, slot, masked=False, stages=(True, True, True)):
        if stages[2]:
            p4(it - 2, slot)
        if stages[1]:
            p23(it - 1, 1 - slot, masked)
        if stages[0]:
            p1(it, slot)

    @pl.when(i == 0)
    def _():
        for t in range(SB_ND):
            p1(t, t % 2)
            p23(t, t % 2, True)
            p4(t, t % 2)

    @pl.when(i > 0)
    def _():
        for it in range(SB_ND + 1):
            trip(it, it % 2, True, (True, it >= 1, it >= 2))

        def cond(carry):
            it, live = carry
            return jnp.logical_and(it < n, live > 0)

        def body(carry):
            it, _ = carry
            for slot in range(2):
                @pl.when(it % 2 == slot)
                def _():
                    trip(it, slot)
            return it + 1, live_ref[0]

        it_end, live = lax.while_loop(cond, body, (jnp.int32(SB_ND + 1), live_ref[0]))
        for slot in range(2):
            @pl.when(it_end % 2 == slot)
            def _():
                p4(it_end - 2, slot)

            @pl.when(jnp.logical_and(it_end % 2 == slot, live > 0))
            def _():
                p23(it_end - 1, 1 - slot, False)
                p4(it_end - 1, 1 - slot)

    o_ref[...] = acc_ref[...].astype(o_ref.dtype)


def _stick_breaking(pm, tri):
    s = pm.shape[0]
    qb, kb, vb = OFF_SQ // 128, OFF_SK // 128, OFF_SV // 128
    return pl.pallas_call(
        _sb_body,
        grid=(SB_HEADS, s // SB_TQ),
        in_specs=[
            pl.BlockSpec((SB_TQ, 128), lambda h, i: (i, qb + h)),
            pl.BlockSpec((s, 128), lambda h, i: (0, kb + h)),
            pl.BlockSpec((s, 128), lambda h, i: (0, vb + h)),
            _const_spec((SB_TK, SB_TK), (0, 0)),
        ],
        out_specs=pl.BlockSpec((SB_TQ, 128), lambda h, i: (i, h)),
        out_shape=jax.ShapeDtypeStruct((s, SB_WIDTH), BF16),
        scratch_shapes=[pltpu.VMEM((SB_TQ, 128), F32), pltpu.VMEM((SB_TQ, 128), F32),
                        pltpu.VMEM((2, SB_TQ, SB_TK), F32), pltpu.VMEM((2, SB_TQ, SB_TK), BF16),
                        pltpu.SMEM((1,), jnp.int32)],
        compiler_params=_params(("parallel", "parallel")),
        name="stick_breaking",
    )(pm, pm, pm, tri)


RET_NC = 8


def _ret_body(q_ref, k_ref, v_ref, g_ref, cos_ref, sin_ref, perm_ref, dec_ref, zeta_ref, xi_ref,
              cd_ref, nw_ref, o_ref, r_ref, *, nc):
    n = pl.program_id(0)

    @pl.when(n == 0)
    def _():
        r_ref[...] = jnp.zeros_like(r_ref)

    cos = jnp.concatenate([cos_ref[...]] * 2, axis=1)
    sin = jnp.concatenate([sin_ref[...]] * 2, axis=1)
    perm = perm_ref[...]

    def rot(x):
        swapped = jnp.dot(x, perm, preferred_element_type=F32)
        return x.astype(F32) * cos + swapped * sin

    rq = rot(q_ref[...])
    rk = rot(k_ref[...])
    qb = rq.astype(BF16)
    qx = (rq * xi_ref[...]).astype(BF16)
    kb = rk.astype(BF16)
    kz = (rk * zeta_ref[...]).astype(BF16)
    lane = lax.broadcasted_iota(jnp.int32, (RET_CHUNK, 128), 1)
    lo = lane < RET_DK
    zero = jnp.zeros((RET_CHUNK, 128), BF16)
    for h in range(RET_HEADS):
        sl = slice((h // 2) * 128, (h // 2 + 1) * 128)
        sel = lo if h % 2 == 0 else jnp.logical_not(lo)
        hs = slice(h * RET_DV, (h + 1) * RET_DV)
        r = r_ref[h]
        for c in range(nc):
            rs = slice(c * RET_CHUNK, (c + 1) * RET_CHUNK)
            v = v_ref[rs, hs]
            qm = jnp.where(sel, qb[rs, sl], zero)
            qxm = jnp.where(sel, qx[rs, sl], zero)
            intra = lax.dot_general(qm, kb[rs, sl], NT_DIMS, preferred_element_type=F32) * dec_ref[h]
            o = jnp.dot(intra.astype(BF16), v, preferred_element_type=F32)
            o = o + jnp.dot(qxm, r.astype(BF16), preferred_element_type=F32)
            kv = lax.dot_general(kz[rs, sl], v, TN_DIMS, preferred_element_type=F32)
            r = r * cd_ref[h] + kv
            mu = jnp.mean(o, axis=-1, keepdims=True)
            d = o - mu
            var = jnp.mean(d * d, axis=-1, keepdims=True)
            on = (d * lax.rsqrt(var + EPS)) * nw_ref[:, hs]
            gate = g_ref[rs, hs].astype(F32)
            o_ref[rs, hs] = (gate * jax.nn.sigmoid(gate) * on).astype(o_ref.dtype)
        r_ref[h] = r


def _retention(pm, tabs, ret_nw3, l):
    s = pm.shape[0]
    cos_t, sin_t, perm, dec, zeta, xi, cd = tabs
    nc = min(RET_NC, s // RET_CHUNK)
    rows = nc * RET_CHUNK
    zeta_t, xi_t = jnp.tile(zeta, (nc, 1)), jnp.tile(xi, (nc, 1))
    return pl.pallas_call(
        functools.partial(_ret_body, nc=nc),
        grid=(s // rows,),
        in_specs=[
            pl.BlockSpec((rows, 256), lambda n: (n, OFF_RQ // 256)),
            pl.BlockSpec((rows, 256), lambda n: (n, OFF_RK // 256)),
            pl.BlockSpec((rows, 512), lambda n: (n, OFF_RV // 512)),
            pl.BlockSpec((rows, 512), lambda n: (n, OFF_RG // 512)),
            pl.BlockSpec((rows, 128), lambda n: (n, 0)),
            pl.BlockSpec((rows, 128), lambda n: (n, 0)),
            _const_spec((256, 256), (0, 0)),
            _const_spec((RET_HEADS, RET_CHUNK, RET_CHUNK), (0, 0, 0)),
            _const_spec((rows, 256), (0, 0)),
            _const_spec((rows, 256), (0, 0)),
            _const_spec((RET_HEADS, 128, RET_DV), (0, 0, 0)),
            _const_spec((None, 1, RET_V_WIDTH), (l, 0, 0)),
        ],
        out_specs=pl.BlockSpec((rows, RET_V_WIDTH), lambda n: (n, 0)),
        out_shape=jax.ShapeDtypeStruct((s, RET_V_WIDTH), BF16),
        scratch_shapes=[pltpu.VMEM((RET_HEADS, 128, RET_DV), F32)],
        compiler_params=_params(("arbitrary",)),
        name="retention",
    )(pm, pm, pm, pm, cos_t, sin_t, perm, dec, zeta_t, xi_t, cd, ret_nw3)


MF_TM = 512
FF_CH = 256
CARRY = 8
MF_VMEM_LIMIT = 60 * 1024 * 1024


def _gelu_tanh(x):
    return 0.5 * x * (1.0 + jnp.tanh(np.sqrt(2.0 / np.pi).astype(np.float32) * (x + 0.044715 * (x * x * x))))


def _rms(x, w):
    return x * lax.rsqrt(jnp.mean(x * x, axis=-1, keepdims=True) + EPS) * w


def _mix_ffn_body(x_ref, o0, o1, o2, l0, l1, l2, yb_ref, yc_ref, gi_ref, bg_ref, wa_ref, wb_ref, wc_ref, wo_ref,
                  nw_ref, wu_ref, wg_ref, cw_ref, cb_ref, wd_ref, fw_ref, *rest, final):
    if final:
        out_ref, so1, so2, sl1, sl2, u_ref, carry_ref, a_ref = rest
    else:
        w_next_ref, out_ref, w_next_bf_ref, so1, so2, sl1, sl2, u_ref, carry_ref, a_ref = rest
        w_next_bf_ref[...] = w_next_ref[...].astype(BF16)
    i = pl.program_id(0)

    @pl.when(i == 0)
    def _():
        carry_ref[...] = jnp.zeros_like(carry_ref)

    for src, dst, d in ((o1, so1, DIL_GROUPS[1][1]), (l1, sl1, DIL_GROUPS[1][1]),
                        (o2, so2, DIL_GROUPS[2][1]), (l2, sl2, DIL_GROUPS[2][1])):
        for r in range(d):
            for j in range(A_OUT // 128):
                col = r * A_OUT + j * 128
                dst[j, pl.ds(r, MF_TM // d, stride=d), :] = src[:, col:col + 128]

    def tok(ref):
        return jnp.concatenate([ref[j] for j in range(A_OUT // 128)], axis=1)

    a0, a1, a2 = l0[...], tok(sl1), tok(sl2)
    m = jnp.maximum(jnp.maximum(a0, a1), a2)
    e0, e1, e2 = jnp.exp(a0 - m), jnp.exp(a1 - m), jnp.exp(a2 - m)
    ya = (e0 * o0[...] + e1 * tok(so1) + e2 * tok(so2)) / (e0 + e1 + e2)
    pa = jnp.dot(ya.astype(BF16), wa_ref[...], preferred_element_type=F32)
    pb = jnp.dot(yb_ref[...], wb_ref[...], preferred_element_type=F32)
    pc = jnp.dot(yc_ref[...], wc_ref[...], preferred_element_type=F32)
    merged = None
    for b, pr in enumerate((pa, pb, pc)):
        gate = jax.nn.sigmoid(gi_ref[:, b * D_MODEL:(b + 1) * D_MODEL].astype(F32) + bg_ref[b:b + 1, :])
        merged = gate * pr if merged is None else merged + gate * pr
    x = x_ref[...] + jnp.dot(merged.astype(BF16), wo_ref[...], preferred_element_type=F32)

    h = _rms(x, nw_ref[...]).astype(BF16)
    for c in range(D_FF // FF_CH):
        sl = slice(c * FF_CH, (c + 1) * FF_CH)
        ub = u_ref.at[c % 2]
        ub[0:CARRY, :] = carry_ref[:, sl]
        ub[CARRY:CARRY + MF_TM, :] = jnp.dot(h, wu_ref[:, sl], preferred_element_type=F32)
        u0 = ub[CARRY:CARRY + MF_TM, :]
        u1 = ub[CARRY - 1:CARRY - 1 + MF_TM, :]
        u2 = ub[CARRY - 2:CARRY - 2 + MF_TM, :]
        carry_ref[:, sl] = ub[MF_TM:MF_TM + CARRY, :]
        uc = cw_ref[0:1, sl] * u2 + cw_ref[1:2, sl] * u1 + cw_ref[2:3, sl] * u0 + cb_ref[:, sl]
        g = jnp.dot(h, wg_ref[:, sl], preferred_element_type=F32)
        a_ref[:, sl] = (_gelu_tanh(uc) * g).astype(BF16)
    y = x + jnp.dot(a_ref[...], wd_ref[...], preferred_element_type=F32)
    out_ref[...] = _rms(y, fw_ref[...]) if final else y


def _mix_ffn(x, oas, lses, yb, yc, gi, b_gate, wa, wb, wc, wo, nw3, wu, wg, cw, cb3, wd, fw, w_in, l):
    s = x.shape[0]
    tm = MF_TM
    final = l == DEPTH - 1
    row = lambda w: pl.BlockSpec((tm, w), lambda i: (i, 0))
    grp = [pl.BlockSpec((tm // d, d * A_OUT), lambda i: (i, 0)) for _, d in DIL_GROUPS]
    cast = None if final else _cast_specs(w_in, l + 1, s // tm)
    return pl.pallas_call(
        functools.partial(_mix_ffn_body, final=final),
        grid=(s // tm,),
        in_specs=[row(D_MODEL)] + grp + grp + [row(SB_WIDTH), row(RET_V_WIDTH), row(GATE_W),
                  _const_spec((None, N_BRANCH, D_MODEL), (l, 0, 0)),
                  _const_spec((A_OUT, D_MODEL), (0, 0)),
                  _const_spec((SB_WIDTH, D_MODEL), (0, 0)),
                  _const_spec((RET_V_WIDTH, D_MODEL), (0, 0)),
                  _const_spec((D_MODEL, D_MODEL), (0, 0)),
                  _const_spec((None, 1, D_MODEL), (l, 0, 0)),
                  _const_spec((D_MODEL, D_FF), (0, 0)),
                  _const_spec((D_MODEL, D_FF), (0, 0)),
                  _const_spec((None, 3, D_FF), (l, 0, 0)),
                  _const_spec((None, 1, D_FF), (l, 0, 0)),
                  _const_spec((D_FF, D_MODEL), (0, 0)),
                  _const_spec((1, D_MODEL), (0, 0))] + ([] if final else [cast[0]]),
        out_specs=[row(D_MODEL)] + ([] if final else [cast[1]]),
        out_shape=[jax.ShapeDtypeStruct((s, D_MODEL), F32)] + ([] if final else [cast[2]]),
        scratch_shapes=[pltpu.VMEM((A_OUT // 128, tm, 128), F32)] * 4 + [
            pltpu.VMEM((2, tm + CARRY, FF_CH), F32), pltpu.VMEM((CARRY, D_FF), F32),
            pltpu.VMEM((tm, D_FF), BF16)],
        compiler_params=_params(("arbitrary",), MF_VMEM_LIMIT),
        name="mix_ffn",
    )(x, *oas, *lses, yb, yc, gi, b_gate, wa, wb, wc, wo, nw3, wu, wg, cw, cb3, wd, fw, *([] if final else [w_in]))


def _t5_causal_bucket(dist):
    max_exact = REL_BUCKETS // 2
    n = np.maximum(dist, 0)
    large = max_exact + (np.log(np.maximum(n, 1) / max_exact) / np.log(REL_MAX_DIST / max_exact)
                         * (REL_BUCKETS - max_exact)).astype(np.int64)
    large = np.minimum(large, REL_BUCKETS - 1)
    return np.where(n < max_exact, n, large).astype(np.int32)


def _dilated_bias(rel_bias, g, window, dilation):
    n_steps = window // dilation
    steps = np.arange(BLOCK)[:, None] + BLOCK - np.arange(2 * BLOCK)[None, :]
    in_band = (steps >= 0) & (steps <= n_steps)
    bucket = _t5_causal_bucket(steps * dilation)
    bias_g = rel_bias[:, g * A_HEADS_PER_GROUP:(g + 1) * A_HEADS_PER_GROUP].astype(F32)
    onehot = (jnp.asarray(bucket)[:, :, None] == jnp.arange(REL_BUCKETS)[None, None, :]).astype(F32)
    bias = jnp.einsum('qkb,bh->hqk', onehot, bias_g, precision=lax.Precision.HIGHEST)
    return jnp.where(jnp.asarray(in_band)[None], bias, NEG)


def _retention_tables(s):
    half = RET_DK // 2
    inv = ROPE_BASE ** (-jnp.arange(half, dtype=F32) / half)
    ang = jnp.arange(s, dtype=jnp.int32).astype(F32)[:, None] * inv[None, :]
    cos, sin = jnp.cos(ang), jnp.sin(ang)
    cos_t = jnp.concatenate([cos, cos, cos, cos], axis=-1)
    sin_t = jnp.concatenate([-sin, sin, -sin, sin], axis=-1)
    j = np.arange(RET_QK_WIDTH)
    partner = np.where((j % RET_DK) < half, j + half, j - half)
    perm = np.zeros((RET_QK_WIDTH, RET_QK_WIDTH), np.float32)
    perm[partner, j] = 1.0
    log_gamma = jnp.log1p(-jnp.exp2(-5.0 - jnp.arange(RET_HEADS, dtype=F32)))
    n = jnp.arange(RET_CHUNK, dtype=F32)
    diff = n[:, None] - n[None, :]
    dec = jnp.where(diff >= 0, jnp.exp(diff[None] * log_gamma[:, None, None]), 0.0)
    zeta = jnp.exp((RET_CHUNK - 1 - n)[:, None] * log_gamma[None, :])
    xi = jnp.exp((n + 1)[:, None] * log_gamma[None, :])
    zeta = jnp.repeat(zeta, RET_DK, axis=1)
    xi = jnp.repeat(xi, RET_DK, axis=1)
    cd = jnp.broadcast_to(jnp.exp(RET_CHUNK * log_gamma)[:, None, None], (RET_HEADS, 128, RET_DV))
    return cos_t, sin_t, jnp.asarray(perm, BF16), dec, zeta, xi, cd


def _scale_row():
    sc = np.ones((1, MAIN_W), np.float32)
    sc[:, 0:A_WIDTH] = HEAD_DIM ** -0.5
    sc[:, A_W + OFF_SQ:A_W + OFF_SQ + SB_WIDTH] = SB_HEAD_DIM ** -0.5 * LOG2E
    sc[:, A_W + OFF_RK:A_W + OFF_RK + RET_QK_WIDTH] = RET_DK ** -0.5
    return jnp.asarray(sc)


def _tri_table():
    k = np.arange(SB_TK)[:, None]
    j = np.arange(SB_TK)[None, :]
    return jnp.asarray(-(k > j).astype(np.float32), BF16)


def kernel(x, rel_bias, norm_mix_w, w_in, b_gate, ret_norm_w, w_proj_a, w_proj_b, w_proj_c, w_out,
           norm_ffn_w, w_up, w_gate, conv_w, conv_b, w_down, final_norm_w):
    b, s, _ = x.shape
    assert b == 1
    xs = x.reshape(s, D_MODEL)
    w_in_bf = w_in[0].astype(BF16)
    side_weights = (w_proj_a, w_proj_b, w_proj_c, w_out, w_up, w_gate, w_down)
    assert len(side_weights) == N_SIDE_CAST
    nmix3 = norm_mix_w.reshape(DEPTH, 1, D_MODEL)
    nffn3 = norm_ffn_w.reshape(DEPTH, 1, D_MODEL)
    rnw3 = ret_norm_w.reshape(DEPTH, 1, RET_V_WIDTH)
    cb3 = conv_b.reshape(DEPTH, 1, D_FF)
    scale_row = _scale_row()
    tri = _tri_table()
    biases = [_dilated_bias(rel_bias, g, w, d) for g, (w, d) in enumerate(DIL_GROUPS)]
    rtabs = _retention_tables(s)

    for l in range(DEPTH):
        a0, a1, a2, pm, gi, wa, wb, wc, wo, wu, wg, wd = _inproj(xs, nmix3, scale_row, w_in_bf, side_weights, l)
        oas, lses = _dilated_attention((a0, a1, a2), biases)
        yb = _stick_breaking(pm, tri)
        yc = _retention(pm, rtabs, rnw3, l)
        res = _mix_ffn(xs, oas, lses, yb, yc, gi, b_gate, wa, wb, wc, wo, nffn3, wu, wg, conv_w, cb3, wd,
                       final_norm_w.reshape(1, D_MODEL), w_in, l)
        xs = res[0]
        if l < DEPTH - 1:
            w_in_bf = res[1]
    return xs.reshape(b, s, D_MODEL)
```

```python
import functools

import numpy as np
import jax
import jax.numpy as jnp
from jax import lax
from jax.experimental import pallas as pl
from jax.experimental.pallas import tpu as pltpu

F32 = jnp.float32
BF16 = jnp.bfloat16

D_MODEL = 1024
DEPTH = 4
HEAD_DIM = 64
BLOCK = 128
DIL_GROUPS = ((128, 1), (512, 4), (2048, 16))
N_GROUPS = 3
A_HEADS_PER_GROUP = 4
A_WIDTH = 768
A_OUT = 256
SB_HEADS = 4
SB_HEAD_DIM = 128
SB_WIDTH = 512
RET_HEADS = 4
RET_DK = 64
RET_DV = 128
RET_QK_WIDTH = 256
RET_V_WIDTH = 512
RET_CHUNK = 128
ROPE_BASE = 10000.0
REL_BUCKETS = 32
REL_MAX_DIST = 2048
N_BRANCH = 3
D_FF = 2816
EPS = 1e-6

GATE_W = N_BRANCH * D_MODEL
A_W = 3 * A_WIDTH
PM_W = 3 * SB_WIDTH + 2 * RET_QK_WIDTH + 2 * RET_V_WIDTH
MAIN_W = A_W + PM_W
IN_WIDTH = MAIN_W + GATE_W
G_W = 3 * A_OUT
OFF_SQ, OFF_SK, OFF_SV = 0, 512, 1024
OFF_RQ, OFF_RK, OFF_RV, OFF_RG = 1536, 1792, 2048, 2560

LANE = 128
BF16_SUBLANES = 16
NEG = -1e30
VMEM_LIMIT = 56 * 1024 * 1024

NT_DIMS = (((1,), (1,)), ((), ()))
TN_DIMS = (((0,), (0,)), ((), ()))


def _params(sem, vmem=VMEM_LIMIT):
    return pltpu.CompilerParams(dimension_semantics=sem, vmem_limit_bytes=vmem)


def _const_spec(shape, index):
    return pl.BlockSpec(shape, lambda *_: index, pipeline_mode=pl.Buffered(1))


IN_TM = 512
IN_CH = 768
N_SIDE_CAST = 7


def _inproj_body(x_ref, nw_ref, sc_ref, w_ref, *rest):
    nc = N_SIDE_CAST
    cast_in, (a0_ref, a1_ref, a2_ref, pm_ref, gi_ref) = rest[:nc], rest[nc:nc + 5]
    cast_out, ys_ref = rest[nc + 5:2 * nc + 5], rest[2 * nc + 5]
    for src, dst in zip(cast_in, cast_out):
        dst[...] = src[...].astype(BF16)
    x = x_ref[...]
    h = (x * lax.rsqrt(jnp.mean(x * x, axis=-1, keepdims=True) + EPS) * nw_ref[...]).astype(BF16)
    a_refs = (a0_ref, a1_ref, a2_ref)
    for c in range(A_W // IN_CH):
        sl = slice(c * IN_CH, (c + 1) * IN_CH)
        y = jnp.dot(h, w_ref[:, sl], preferred_element_type=F32) * sc_ref[:, sl]
        for j in range(IN_CH // LANE):
            ys_ref[j] = y[:, j * LANE:(j + 1) * LANE]
        for g, (_, d) in enumerate(DIL_GROUPS):
            for r in range(d):
                rows = pl.ds(r, IN_TM // d, stride=d) if d > 1 else slice(None)
                for j in range(A_OUT // LANE):
                    col = r * G_W + c * A_OUT + j * LANE
                    a_refs[g][:, col:col + LANE] = ys_ref[g * (A_OUT // LANE) + j, rows, :].astype(BF16)
    for c in range(PM_W // IN_CH):
        sl = slice(A_W + c * IN_CH, A_W + (c + 1) * IN_CH)
        y = jnp.dot(h, w_ref[:, sl], preferred_element_type=F32)
        pm_ref[:, c * IN_CH:(c + 1) * IN_CH] = (y * sc_ref[:, sl]).astype(BF16)
    for c in range(GATE_W // IN_CH):
        y = jnp.dot(h, w_ref[:, MAIN_W + c * IN_CH:MAIN_W + (c + 1) * IN_CH], preferred_element_type=F32)
        gi_ref[:, c * IN_CH:(c + 1) * IN_CH] = y.astype(BF16)


def _cast_specs(w, l, steps):
    _, rows, cols = w.shape
    k = 1
    while (rows * k) % steps or (rows * k // steps) % BF16_SUBLANES:
        k *= 2
    br = rows * k // steps
    return (pl.BlockSpec((None, br, cols), lambda i: (l, i // k, 0)),
            pl.BlockSpec((br, cols), lambda i: (i // k, 0)),
            jax.ShapeDtypeStruct((rows, cols), BF16))


def _inproj(x, norm_w3, scale_row, w_in_bf, side_weights, l):
    s = x.shape[0]
    steps = s // IN_TM
    a_shapes = [jax.ShapeDtypeStruct((s // d, d * G_W), BF16) for _, d in DIL_GROUPS]
    a_specs = [pl.BlockSpec((IN_TM // d, d * G_W), lambda i: (i, 0)) for _, d in DIL_GROUPS]
    cast = [_cast_specs(w, l, steps) for w in side_weights]
    return pl.pallas_call(
        _inproj_body,
        grid=(steps,),
        in_specs=[
            pl.BlockSpec((IN_TM, D_MODEL), lambda i: (i, 0)),
            _const_spec((None, 1, D_MODEL), (l, 0, 0)),
            _const_spec((1, MAIN_W), (0, 0)),
            _const_spec((D_MODEL, IN_WIDTH), (0, 0)),
        ] + [c[0] for c in cast],
        out_specs=a_specs + [
            pl.BlockSpec((IN_TM, PM_W), lambda i: (i, 0)),
            pl.BlockSpec((IN_TM, GATE_W), lambda i: (i, 0)),
        ] + [c[1] for c in cast],
        out_shape=a_shapes + [jax.ShapeDtypeStruct((s, PM_W), BF16), jax.ShapeDtypeStruct((s, GATE_W), BF16)]
        + [c[2] for c in cast],
        scratch_shapes=[pltpu.VMEM((IN_CH // LANE, IN_TM, LANE), F32)],
        compiler_params=_params(("arbitrary",)),
        name="inproj",
    )(x, norm_w3, scale_row, w_in_bf, *side_weights)


DIL_STEPS = max(d for _, d in DIL_GROUPS)


def _dil_group(n, q_ref, k_ref, ke_ref, v_ref, ve_ref, bias_ref, o_ref, lse_ref, nblk):
    lane = lax.broadcasted_iota(jnp.int32, (BLOCK, LANE), 1)
    lo = lane < HEAD_DIM
    col = lax.broadcasted_iota(jnp.int32, (BLOCK, 2 * BLOCK), 1)
    first_ok = jnp.logical_or(n > 0, col >= BLOCK)
    for b in range(nblk):
        for p in range(2):
            sl = slice(p * LANE, (p + 1) * LANE)
            q = q_ref[b * BLOCK:(b + 1) * BLOCK, sl]
            if b == 0:
                k2 = jnp.concatenate([ke_ref[:, sl], k_ref[0:BLOCK, sl]], axis=0)
                v2 = jnp.concatenate([ve_ref[:, sl], v_ref[0:BLOCK, sl]], axis=0)
            else:
                k2 = k_ref[(b - 1) * BLOCK:(b + 1) * BLOCK, sl]
                v2 = v_ref[(b - 1) * BLOCK:(b + 1) * BLOCK, sl]
            outs, lses = [], []
            for hh in range(2):
                h = 2 * p + hh
                sel = lo if hh == 0 else jnp.logical_not(lo)
                qm = jnp.where(sel, q, jnp.zeros_like(q))
                s = lax.dot_general(qm, k2, NT_DIMS, preferred_element_type=F32) + bias_ref[h]
                if b == 0:
                    s = jnp.where(first_ok, s, NEG)
                m = jnp.max(s, axis=-1, keepdims=True)
                e = jnp.exp(s - m)
                den = jnp.sum(e, axis=-1, keepdims=True)
                num = jnp.dot(e.astype(BF16), v2, preferred_element_type=F32)
                outs.append(num / den)
                lses.append(jnp.broadcast_to(m + jnp.log(den), (BLOCK, LANE)))
            o_ref[b * BLOCK:(b + 1) * BLOCK, sl] = jnp.where(lo, outs[0], outs[1])
            lse_ref[b * BLOCK:(b + 1) * BLOCK, sl] = jnp.where(lo, lses[0], lses[1])


def _dil_body(*refs, nblk):
    j = pl.program_id(0)
    ng = len(DIL_GROUPS)
    for g, (_, d) in enumerate(DIL_GROUPS):
        n = j % (DIL_STEPS // d)
        _dil_group(n, *refs[6 * g:6 * g + 6], *refs[6 * ng + 2 * g:6 * ng + 2 * g + 2], nblk)


def _dilated_attention(a_groups, biases):
    s = a_groups[0].shape[0] * DIL_GROUPS[0][1]
    tr = s // DIL_STEPS
    nblk = tr // BLOCK
    cb = G_W // A_OUT
    in_specs, out_specs, out_shape = [], [], []
    for (_, d) in DIL_GROUPS:
        per = DIL_STEPS // d

        def cur(off, per=per):
            return pl.BlockSpec((tr, A_OUT), lambda j: (j % per, (j // per) * cb + off))

        def edge(off, per=per):
            return pl.BlockSpec((BLOCK, A_OUT),
                                lambda j: (jnp.maximum((j % per) * nblk - 1, 0), (j // per) * cb + off))

        in_specs += [cur(0), cur(1), edge(1), cur(2), edge(2),
                     _const_spec((A_HEADS_PER_GROUP, BLOCK, 2 * BLOCK), (0, 0, 0))]
        out_specs += [pl.BlockSpec((tr, A_OUT), lambda j, per=per: (j % per, j // per))] * 2
        out_shape += [jax.ShapeDtypeStruct((s // d, d * A_OUT), F32)] * 2
    args = []
    for a_g, bias_g in zip(a_groups, biases):
        args += [a_g] * 5 + [bias_g]
    outs = pl.pallas_call(
        functools.partial(_dil_body, nblk=nblk),
        grid=(DIL_STEPS,),
        in_specs=in_specs,
        out_specs=out_specs,
        out_shape=out_shape,
        compiler_params=_params(("parallel",)),
        name="dilated_attn",
    )(*args)
    return outs[0::2], outs[1::2]


SB_TQ = 1024
SB_TK = 256
SB_ND = SB_TQ // SB_TK
LOG2E = 1.4426950408889634
SB_SOFTPLUS_LINEAR = 64.0
SB_UNDERFLOW_LOG2 = -160.0


def _sb_body(q_ref, k_ref, v_ref, tri_ref, o_ref, acc_ref, c_ref, z_ref, w_ref, live_ref):
    i = pl.program_id(1)
    qi = i * SB_TQ
    acc_ref[...] = jnp.zeros_like(acc_ref)
    c_ref[...] = jnp.zeros_like(c_ref)

    def key_start(t):
        return pl.multiple_of(qi + (SB_ND - 1 - t) * SB_TK, SB_TK)

    def first_row(t):
        return max(0, SB_ND - 1 - t) * SB_TK if isinstance(t, int) else 0

    def p1(t, slot):
        r0 = first_row(t)
        kb = k_ref[pl.ds(key_start(t), SB_TK), :]
        z_ref[slot, r0:, :] = lax.dot_general(q_ref[r0:, :], kb, NT_DIMS, preferred_element_type=F32)

    def p23(t, slot, masked):
        r0 = first_row(t)
        z = z_ref[slot, r0:, :]
        sp = jnp.where(z > SB_SOFTPLUS_LINEAR, z, jnp.log2(1.0 + jnp.exp2(z)))
        c = c_ref[r0:, :]
        u0 = z - sp + jnp.concatenate([c, c], axis=1)
        if masked:
            shape = (SB_TQ - r0, SB_TK)
            row = lax.broadcasted_iota(jnp.int32, shape, 0) + r0
            col = lax.broadcasted_iota(jnp.int32, shape, 1)
            mask = (col + (SB_ND - 1 - t) * SB_TK) < row
            sp = jnp.where(mask, sp, 0.0)
            u0 = jnp.where(mask, u0, NEG)
        c_ref[r0:, :] = c - jnp.sum(sp, axis=-1, keepdims=True)
        live_ref[0] = (jnp.max(c_ref[r0:, :]) > SB_UNDERFLOW_LOG2).astype(jnp.int32)
        u = u0 + jnp.dot(sp.astype(BF16), tri_ref[...], preferred_element_type=F32)
        w_ref[slot, r0:, :] = jnp.exp2(u.astype(BF16))

    def p4(t, slot):
        r0 = first_row(t)
        vb = v_ref[pl.ds(key_start(t), SB_TK), :]
        acc_ref[r0:, :] += jnp.dot(w_ref[slot, r0:, :], vb, preferred_element_type=F32)

    n = (i + 1) * SB_ND

    def trip(it, slot, masked=False, stages=(True, True, True)):
        if stages[2]:
            p4(it - 2, slot)
        if stages[1]:
            p23(it - 1, 1 - slot, masked)
        if stages[0]:
            p1(it, slot)

    @pl.when(i == 0)
    def _():
        for t in range(SB_ND):
            p1(t, t % 2)
            p23(t, t % 2, True)
            p4(t, t % 2)

    @pl.when(i > 0)
    def _():
        for it in range(SB_ND + 1):
            trip(it, it % 2, True, (True, it >= 1, it >= 2))

        def cond(carry):
            it, live = carry
            return jnp.logical_and(it < n, live > 0)

        def body(carry):
            it, _ = carry
            for slot in range(2):
                @pl.when(it % 2 == slot)
                def _():
                    trip(it, slot)
            return it + 1, live_ref[0]

        it_end, live = lax.while_loop(cond, body, (jnp.int32(SB_ND + 1), live_ref[0]))
        for slot in range(2):
            @pl.when(it_end % 2 == slot)
            def _():
                p4(it_end - 2, slot)

            @pl.when(jnp.logical_and(it_end % 2 == slot, live > 0))
            def _():
                p23(it_end - 1, 1 - slot, False)
                p4(it_end - 1, 1 - slot)

    o_ref[...] = acc_ref[...].astype(o_ref.dtype)


def _stick_breaking(pm, tri):
    s = pm.shape[0]
    qb, kb, vb = OFF_SQ // SB_HEAD_DIM, OFF_SK // SB_HEAD_DIM, OFF_SV // SB_HEAD_DIM
    return pl.pallas_call(
        _sb_body,
        grid=(SB_HEADS, s // SB_TQ),
        in_specs=[
            pl.BlockSpec((SB_TQ, SB_HEAD_DIM), lambda h, i: (i, qb + h)),
            pl.BlockSpec((s, SB_HEAD_DIM), lambda h, i: (0, kb + h)),
            pl.BlockSpec((s, SB_HEAD_DIM), lambda h, i: (0, vb + h)),
            _const_spec((SB_TK, SB_TK), (0, 0)),
        ],
        out_specs=pl.BlockSpec((SB_TQ, SB_HEAD_DIM), lambda h, i: (i, h)),
        out_shape=jax.ShapeDtypeStruct((s, SB_WIDTH), BF16),
        scratch_shapes=[pltpu.VMEM((SB_TQ, SB_HEAD_DIM), F32), pltpu.VMEM((SB_TQ, LANE), F32),
                        pltpu.VMEM((2, SB_TQ, SB_TK), F32), pltpu.VMEM((2, SB_TQ, SB_TK), BF16),
                        pltpu.SMEM((1,), jnp.int32)],
        compiler_params=_params(("parallel", "parallel")),
        name="stick_breaking",
    )(pm, pm, pm, tri)


RET_NC = 8


def _ret_body(q_ref, k_ref, v_ref, g_ref, cos_ref, sin_ref, perm_ref, dec_ref, zeta_ref, xi_ref,
              cd_ref, nw_ref, o_ref, r_ref, *, nc):
    n = pl.program_id(0)

    @pl.when(n == 0)
    def _():
        r_ref[...] = jnp.zeros_like(r_ref)

    cos = jnp.concatenate([cos_ref[...]] * 2, axis=1)
    sin = jnp.concatenate([sin_ref[...]] * 2, axis=1)
    perm = perm_ref[...]

    def rot(x):
        swapped = jnp.dot(x, perm, preferred_element_type=F32)
        return x.astype(F32) * cos + swapped * sin

    rq = rot(q_ref[...])
    rk = rot(k_ref[...])
    qb = rq.astype(BF16)
    qx = (rq * xi_ref[...]).astype(BF16)
    kb = rk.astype(BF16)
    kz = (rk * zeta_ref[...]).astype(BF16)
    lane = lax.broadcasted_iota(jnp.int32, (RET_CHUNK, LANE), 1)
    lo = lane < RET_DK
    zero = jnp.zeros((RET_CHUNK, LANE), BF16)
    for h in range(RET_HEADS):
        sl = slice((h // 2) * LANE, (h // 2 + 1) * LANE)
        sel = lo if h % 2 == 0 else jnp.logical_not(lo)
        hs = slice(h * RET_DV, (h + 1) * RET_DV)
        r = r_ref[h]
        for c in range(nc):
            rs = slice(c * RET_CHUNK, (c + 1) * RET_CHUNK)
            v = v_ref[rs, hs]
            qm = jnp.where(sel, qb[rs, sl], zero)
            qxm = jnp.where(sel, qx[rs, sl], zero)
            intra = lax.dot_general(qm, kb[rs, sl], NT_DIMS, preferred_element_type=F32) * dec_ref[h]
            o = jnp.dot(intra.astype(BF16), v, preferred_element_type=F32)
            o = o + jnp.dot(qxm, r.astype(BF16), preferred_element_type=F32)
            kv = lax.dot_general(kz[rs, sl], v, TN_DIMS, preferred_element_type=F32)
            r = r * cd_ref[h] + kv
            mu = jnp.mean(o, axis=-1, keepdims=True)
            d = o - mu
            var = jnp.mean(d * d, axis=-1, keepdims=True)
            on = (d * lax.rsqrt(var + EPS)) * nw_ref[:, hs]
            gate = g_ref[rs, hs].astype(F32)
            o_ref[rs, hs] = (gate * jax.nn.sigmoid(gate) * on).astype(o_ref.dtype)
        r_ref[h] = r


def _retention(pm, tabs, ret_nw3, l):
    s = pm.shape[0]
    cos_t, sin_t, perm, dec, zeta, xi, cd = tabs
    nc = min(RET_NC, s // RET_CHUNK)
    rows = nc * RET_CHUNK
    zeta_t, xi_t = jnp.tile(zeta, (nc, 1)), jnp.tile(xi, (nc, 1))
    return pl.pallas_call(
        functools.partial(_ret_body, nc=nc),
        grid=(s // rows,),
        in_specs=[
            pl.BlockSpec((rows, RET_QK_WIDTH), lambda n: (n, OFF_RQ // RET_QK_WIDTH)),
            pl.BlockSpec((rows, RET_QK_WIDTH), lambda n: (n, OFF_RK // RET_QK_WIDTH)),
            pl.BlockSpec((rows, RET_V_WIDTH), lambda n: (n, OFF_RV // RET_V_WIDTH)),
            pl.BlockSpec((rows, RET_V_WIDTH), lambda n: (n, OFF_RG // RET_V_WIDTH)),
            pl.BlockSpec((rows, LANE), lambda n: (n, 0)),
            pl.BlockSpec((rows, LANE), lambda n: (n, 0)),
            _const_spec((RET_QK_WIDTH, RET_QK_WIDTH), (0, 0)),
            _const_spec((RET_HEADS, RET_CHUNK, RET_CHUNK), (0, 0, 0)),
            _const_spec((rows, RET_QK_WIDTH), (0, 0)),
            _const_spec((rows, RET_QK_WIDTH), (0, 0)),
            _const_spec((RET_HEADS, LANE, RET_DV), (0, 0, 0)),
            _const_spec((None, 1, RET_V_WIDTH), (l, 0, 0)),
        ],
        out_specs=pl.BlockSpec((rows, RET_V_WIDTH), lambda n: (n, 0)),
        out_shape=jax.ShapeDtypeStruct((s, RET_V_WIDTH), BF16),
        scratch_shapes=[pltpu.VMEM((RET_HEADS, LANE, RET_DV), F32)],
        compiler_params=_params(("arbitrary",)),
        name="retention",
    )(pm, pm, pm, pm, cos_t, sin_t, perm, dec, zeta_t, xi_t, cd, ret_nw3)


MF_TM = 512
FF_CH = 256
CARRY = 8
MF_VMEM_LIMIT = 60 * 1024 * 1024


def _gelu_tanh(x):
    return 0.5 * x * (1.0 + jnp.tanh(np.sqrt(2.0 / np.pi).astype(np.float32) * (x + 0.044715 * (x * x * x))))


def _rms(x, w):
    return x * lax.rsqrt(jnp.mean(x * x, axis=-1, keepdims=True) + EPS) * w


def _mix_ffn_body(x_ref, o0, o1, o2, l0, l1, l2, yb_ref, yc_ref, gi_ref, bg_ref, wa_ref, wb_ref, wc_ref, wo_ref,
                  nw_ref, wu_ref, wg_ref, cw_ref, cb_ref, wd_ref, fw_ref, *rest, final):
    if final:
        out_ref, so1, so2, sl1, sl2, u_ref, carry_ref, a_ref = rest
    else:
        w_next_ref, out_ref, w_next_bf_ref, so1, so2, sl1, sl2, u_ref, carry_ref, a_ref = rest
        w_next_bf_ref[...] = w_next_ref[...].astype(BF16)
    i = pl.program_id(0)

    @pl.when(i == 0)
    def _():
        carry_ref[...] = jnp.zeros_like(carry_ref)

    for src, dst, d in ((o1, so1, DIL_GROUPS[1][1]), (l1, sl1, DIL_GROUPS[1][1]),
                        (o2, so2, DIL_GROUPS[2][1]), (l2, sl2, DIL_GROUPS[2][1])):
        for r in range(d):
            for j in range(A_OUT // LANE):
                col = r * A_OUT + j * LANE
                dst[j, pl.ds(r, MF_TM // d, stride=d), :] = src[:, col:col + LANE]

    def tok(ref):
        return jnp.concatenate([ref[j] for j in range(A_OUT // LANE)], axis=1)

    a0, a1, a2 = l0[...], tok(sl1), tok(sl2)
    m = jnp.maximum(jnp.maximum(a0, a1), a2)
    e0, e1, e2 = jnp.exp(a0 - m), jnp.exp(a1 - m), jnp.exp(a2 - m)
    ya = (e0 * o0[...] + e1 * tok(so1) + e2 * tok(so2)) / (e0 + e1 + e2)
    pa = jnp.dot(ya.astype(BF16), wa_ref[...], preferred_element_type=F32)
    pb = jnp.dot(yb_ref[...], wb_ref[...], preferred_element_type=F32)
    pc = jnp.dot(yc_ref[...], wc_ref[...], preferred_element_type=F32)
    merged = None
    for b, pr in enumerate((pa, pb, pc)):
        gate = jax.nn.sigmoid(gi_ref[:, b * D_MODEL:(b + 1) * D_MODEL].astype(F32) + bg_ref[b:b + 1, :])
        merged = gate * pr if merged is None else merged + gate * pr
    x = x_ref[...] + jnp.dot(merged.astype(BF16), wo_ref[...], preferred_element_type=F32)

    h = _rms(x, nw_ref[...]).astype(BF16)
    for c in range(D_FF // FF_CH):
        sl = slice(c * FF_CH, (c + 1) * FF_CH)
        ub = u_ref.at[c % 2]
        ub[0:CARRY, :] = carry_ref[:, sl]
        ub[CARRY:CARRY + MF_TM, :] = jnp.dot(h, wu_ref[:, sl], preferred_element_type=F32)
        u0 = ub[CARRY:CARRY + MF_TM, :]
        u1 = ub[CARRY - 1:CARRY - 1 + MF_TM, :]
        u2 = ub[CARRY - 2:CARRY - 2 + MF_TM, :]
        carry_ref[:, sl] = ub[MF_TM:MF_TM + CARRY, :]
        uc = cw_ref[0:1, sl] * u2 + cw_ref[1:2, sl] * u1 + cw_ref[2:3, sl] * u0 + cb_ref[:, sl]
        g = jnp.dot(h, wg_ref[:, sl], preferred_element_type=F32)
        a_ref[:, sl] = (_gelu_tanh(uc) * g).astype(BF16)
    y = x + jnp.dot(a_ref[...], wd_ref[...], preferred_element_type=F32)
    out_ref[...] = _rms(y, fw_ref[...]) if final else y


def _mix_ffn(x, oas, lses, yb, yc, gi, b_gate, wa, wb, wc, wo, nw3, wu, wg, cw, cb3, wd, fw, w_in, l):
    s = x.shape[0]
    tm = MF_TM
    final = l == DEPTH - 1
    row = lambda w: pl.BlockSpec((tm, w), lambda i: (i, 0))
    grp = [pl.BlockSpec((tm // d, d * A_OUT), lambda i: (i, 0)) for _, d in DIL_GROUPS]
    cast = None if final else _cast_specs(w_in, l + 1, s // tm)
    return pl.pallas_call(
        functools.partial(_mix_ffn_body, final=final),
        grid=(s // tm,),
        in_specs=[row(D_MODEL)] + grp + grp + [row(SB_WIDTH), row(RET_V_WIDTH), row(GATE_W),
                  _const_spec((None, N_BRANCH, D_MODEL), (l, 0, 0)),
                  _const_spec((A_OUT, D_MODEL), (0, 0)),
                  _const_spec((SB_WIDTH, D_MODEL), (0, 0)),
                  _const_spec((RET_V_WIDTH, D_MODEL), (0, 0)),
                  _const_spec((D_MODEL, D_MODEL), (0, 0)),
                  _const_spec((None, 1, D_MODEL), (l, 0, 0)),
                  _const_spec((D_MODEL, D_FF), (0, 0)),
                  _const_spec((D_MODEL, D_FF), (0, 0)),
                  _const_spec((None, 3, D_FF), (l, 0, 0)),
                  _const_spec((None, 1, D_FF), (l, 0, 0)),
                  _const_spec((D_FF, D_MODEL), (0, 0)),
                  _const_spec((1, D_MODEL), (0, 0))] + ([] if final else [cast[0]]),
        out_specs=[row(D_MODEL)] + ([] if final else [cast[1]]),
        out_shape=[jax.ShapeDtypeStruct((s, D_MODEL), F32)] + ([] if final else [cast[2]]),
        scratch_shapes=[pltpu.VMEM((A_OUT // LANE, tm, LANE), F32)] * 4 + [
            pltpu.VMEM((2, tm + CARRY, FF_CH), F32), pltpu.VMEM((CARRY, D_FF), F32),
            pltpu.VMEM((tm, D_FF), BF16)],
        compiler_params=_params(("arbitrary",), MF_VMEM_LIMIT),
        name="mix_ffn",
    )(x, *oas, *lses, yb, yc, gi, b_gate, wa, wb, wc, wo, nw3, wu, wg, cw, cb3, wd, fw, *([] if final else [w_in]))


def _t5_causal_bucket(dist):
    max_exact = REL_BUCKETS // 2
    n = np.maximum(dist, 0)
    large = max_exact + (np.log(np.maximum(n, 1) / max_exact) / np.log(REL_MAX_DIST / max_exact)
                         * (REL_BUCKETS - max_exact)).astype(np.int64)
    large = np.minimum(large, REL_BUCKETS - 1)
    return np.where(n < max_exact, n, large).astype(np.int32)


def _dilated_bias(rel_bias, g, window, dilation):
    n_steps = window // dilation
    steps = np.arange(BLOCK)[:, None] + BLOCK - np.arange(2 * BLOCK)[None, :]
    in_band = (steps >= 0) & (steps <= n_steps)
    bucket = _t5_causal_bucket(steps * dilation)
    bias_g = rel_bias[:, g * A_HEADS_PER_GROUP:(g + 1) * A_HEADS_PER_GROUP].astype(F32)
    onehot = (jnp.asarray(bucket)[:, :, None] == jnp.arange(REL_BUCKETS)[None, None, :]).astype(F32)
    bias = jnp.einsum('qkb,bh->hqk', onehot, bias_g, precision=lax.Precision.HIGHEST)
    return jnp.where(jnp.asarray(in_band)[None], bias, NEG)


def _retention_tables(s):
    half = RET_DK // 2
    inv = ROPE_BASE ** (-jnp.arange(half, dtype=F32) / half)
    ang = jnp.arange(s, dtype=jnp.int32).astype(F32)[:, None] * inv[None, :]
    cos, sin = jnp.cos(ang), jnp.sin(ang)
    cos_t = jnp.concatenate([cos, cos, cos, cos], axis=-1)
    sin_t = jnp.concatenate([-sin, sin, -sin, sin], axis=-1)
    j = np.arange(RET_QK_WIDTH)
    partner = np.where((j % RET_DK) < half, j + half, j - half)
    perm = np.zeros((RET_QK_WIDTH, RET_QK_WIDTH), np.float32)
    perm[partner, j] = 1.0
    log_gamma = jnp.log1p(-jnp.exp2(-5.0 - jnp.arange(RET_HEADS, dtype=F32)))
    n = jnp.arange(RET_CHUNK, dtype=F32)
    diff = n[:, None] - n[None, :]
    dec = jnp.where(diff >= 0, jnp.exp(diff[None] * log_gamma[:, None, None]), 0.0)
    zeta = jnp.exp((RET_CHUNK - 1 - n)[:, None] * log_gamma[None, :])
    xi = jnp.exp((n + 1)[:, None] * log_gamma[None, :])
    zeta = jnp.repeat(zeta, RET_DK, axis=1)
    xi = jnp.repeat(xi, RET_DK, axis=1)
    cd = jnp.broadcast_to(jnp.exp(RET_CHUNK * log_gamma)[:, None, None], (RET_HEADS, LANE, RET_DV))
    return cos_t, sin_t, jnp.asarray(perm, BF16), dec, zeta, xi, cd


def _scale_row():
    sc = np.ones((1, MAIN_W), np.float32)
    sc[:, 0:A_WIDTH] = HEAD_DIM ** -0.5
    sc[:, A_W + OFF_SQ:A_W + OFF_SQ + SB_WIDTH] = SB_HEAD_DIM ** -0.5 * LOG2E
    sc[:, A_W + OFF_RK:A_W + OFF_RK + RET_QK_WIDTH] = RET_DK ** -0.5
    return jnp.asarray(sc)


def _tri_table():
    k = np.arange(SB_TK)[:, None]
    j = np.arange(SB_TK)[None, :]
    return jnp.asarray(-(k > j).astype(np.float32), BF16)


def kernel(x, rel_bias, norm_mix_w, w_in, b_gate, ret_norm_w, w_proj_a, w_proj_b, w_proj_c, w_out,
           norm_ffn_w, w_up, w_gate, conv_w, conv_b, w_down, final_norm_w):
    b, s, _ = x.shape
    assert b == 1
    xs = x.reshape(s, D_MODEL)
    w_in_bf = w_in[0].astype(BF16)
    side_weights = (w_proj_a, w_proj_b, w_proj_c, w_out, w_up, w_gate, w_down)
    assert len(side_weights) == N_SIDE_CAST
    nmix3 = norm_mix_w.reshape(DEPTH, 1, D_MODEL)
    nffn3 = norm_ffn_w.reshape(DEPTH, 1, D_MODEL)
    rnw3 = ret_norm_w.reshape(DEPTH, 1, RET_V_WIDTH)
    cb3 = conv_b.reshape(DEPTH, 1, D_FF)
    scale_row = _scale_row()
    tri = _tri_table()
    biases = [_dilated_bias(rel_bias, g, w, d) for g, (w, d) in enumerate(DIL_GROUPS)]
    rtabs = _retention_tables(s)

    for l in range(DEPTH):
        a0, a1, a2, pm, gi, wa, wb, wc, wo, wu, wg, wd = _inproj(xs, nmix3, scale_row, w_in_bf, side_weights, l)
        oas, lses = _dilated_attention((a0, a1, a2), biases)
        yb = _stick_breaking(pm, tri)
        yc = _retention(pm, rtabs, rnw3, l)
        res = _mix_ffn(xs, oas, lses, yb, yc, gi, b_gate, wa, wb, wc, wo, nffn3, wu, wg, conv_w, cb3, wd,
                       final_norm_w.reshape(1, D_MODEL), w_in, l)
        xs = res[0]
        if l < DEPTH - 1:
            w_in_bf = res[1]
    return xs.reshape(b, s, D_MODEL)
```

```python
import functools

import numpy as np
import jax
import jax.numpy as jnp
from jax import lax
from jax.experimental import pallas as pl
from jax.experimental.pallas import tpu as pltpu

F32 = jnp.float32
BF16 = jnp.bfloat16

D_MODEL = 1024
DEPTH = 4
HEAD_DIM = 64
BLOCK = 128
DIL_GROUPS = ((128, 1), (512, 4), (2048, 16))
N_GROUPS = 3
A_HEADS_PER_GROUP = 4
A_WIDTH = 768
A_OUT = 256
SB_HEADS = 4
SB_HEAD_DIM = 128
SB_WIDTH = 512
RET_HEADS = 4
RET_DK = 64
RET_DV = 128
RET_QK_WIDTH = 256
RET_V_WIDTH = 512
RET_CHUNK = 128
ROPE_BASE = 10000.0
REL_BUCKETS = 32
REL_MAX_DIST = 2048
N_BRANCH = 3
D_FF = 2816
EPS = 1e-6

GATE_W = N_BRANCH * D_MODEL
A_W = 3 * A_WIDTH
PM_W = 3 * SB_WIDTH + 2 * RET_QK_WIDTH + 2 * RET_V_WIDTH
MAIN_W = A_W + PM_W
IN_WIDTH = MAIN_W + GATE_W
G_W = 3 * A_OUT
OFF_SQ, OFF_SK, OFF_SV = 0, 512, 1024
OFF_RQ, OFF_RK, OFF_RV, OFF_RG = 1536, 1792, 2048, 2560

LANE = 128
BF16_SUBLANES = 16
NEG = -1e30
VMEM_LIMIT = 56 * 1024 * 1024

NT_DIMS = (((1,), (1,)), ((), ()))
TN_DIMS = (((0,), (0,)), ((), ()))


def _params(sem, vmem=VMEM_LIMIT):
    return pltpu.CompilerParams(dimension_semantics=sem, vmem_limit_bytes=vmem)


def _const_spec(shape, index):
    return pl.BlockSpec(shape, lambda *_: index, pipeline_mode=pl.Buffered(1))


IN_TM = 512
IN_CH = 768
N_SIDE_CAST = 7


def _inproj_body(x_ref, nw_ref, sc_ref, w_ref, *rest):
    nc = N_SIDE_CAST
    cast_in, (a0_ref, a1_ref, a2_ref, pm_ref) = rest[:nc], rest[nc:nc + 4]
    cast_out, ys_ref = rest[nc + 4:2 * nc + 4], rest[2 * nc + 4]
    for src, dst in zip(cast_in, cast_out):
        dst[...] = src[...].astype(BF16)
    x = x_ref[...]
    h = (x * lax.rsqrt(jnp.mean(x * x, axis=-1, keepdims=True) + EPS) * nw_ref[...]).astype(BF16)
    a_refs = (a0_ref, a1_ref, a2_ref)
    for c in range(A_W // IN_CH):
        sl = slice(c * IN_CH, (c + 1) * IN_CH)
        y = jnp.dot(h, w_ref[:, sl], preferred_element_type=F32) * sc_ref[:, sl]
        for j in range(IN_CH // LANE):
            ys_ref[j] = y[:, j * LANE:(j + 1) * LANE]
        for g, (_, d) in enumerate(DIL_GROUPS):
            for r in range(d):
                rows = pl.ds(r, IN_TM // d, stride=d) if d > 1 else slice(None)
                for j in range(A_OUT // LANE):
                    col = r * G_W + c * A_OUT + j * LANE
                    a_refs[g][:, col:col + LANE] = ys_ref[g * (A_OUT // LANE) + j, rows, :].astype(BF16)
    for c in range(PM_W // IN_CH):
        sl = slice(A_W + c * IN_CH, A_W + (c + 1) * IN_CH)
        y = jnp.dot(h, w_ref[:, sl], preferred_element_type=F32)
        pm_ref[:, c * IN_CH:(c + 1) * IN_CH] = (y * sc_ref[:, sl]).astype(BF16)


def _cast_specs(w, l, steps):
    _, rows, cols = w.shape
    k = 1
    while (rows * k) % steps or (rows * k // steps) % BF16_SUBLANES:
        k *= 2
    br = rows * k // steps
    return (pl.BlockSpec((None, br, cols), lambda i: (l, i // k, 0)),
            pl.BlockSpec((br, cols), lambda i: (i // k, 0)),
            jax.ShapeDtypeStruct((rows, cols), BF16))


def _inproj(x, norm_w3, scale_row, w_main_bf, side_weights, l):
    s = x.shape[0]
    steps = s // IN_TM
    a_shapes = [jax.ShapeDtypeStruct((s // d, d * G_W), BF16) for _, d in DIL_GROUPS]
    a_specs = [pl.BlockSpec((IN_TM // d, d * G_W), lambda i: (i, 0)) for _, d in DIL_GROUPS]
    cast = [_cast_specs(w, l, steps) for w in side_weights]
    return pl.pallas_call(
        _inproj_body,
        grid=(steps,),
        in_specs=[
            pl.BlockSpec((IN_TM, D_MODEL), lambda i: (i, 0)),
            _const_spec((None, 1, D_MODEL), (l, 0, 0)),
            _const_spec((1, MAIN_W), (0, 0)),
            _const_spec((D_MODEL, MAIN_W), (0, 0)),
        ] + [c[0] for c in cast],
        out_specs=a_specs + [pl.BlockSpec((IN_TM, PM_W), lambda i: (i, 0))] + [c[1] for c in cast],
        out_shape=a_shapes + [jax.ShapeDtypeStruct((s, PM_W), BF16)] + [c[2] for c in cast],
        scratch_shapes=[pltpu.VMEM((IN_CH // LANE, IN_TM, LANE), F32)],
        compiler_params=_params(("arbitrary",)),
        name="inproj",
    )(x, norm_w3, scale_row, w_main_bf, *side_weights)


DIL_STEPS = max(d for _, d in DIL_GROUPS)


def _dil_group(n, q_ref, k_ref, ke_ref, v_ref, ve_ref, bias_ref, o_ref, lse_ref, nblk):
    lane = lax.broadcasted_iota(jnp.int32, (BLOCK, LANE), 1)
    lo = lane < HEAD_DIM
    col = lax.broadcasted_iota(jnp.int32, (BLOCK, 2 * BLOCK), 1)
    first_ok = jnp.logical_or(n > 0, col >= BLOCK)
    for b in range(nblk):
        for p in range(2):
            sl = slice(p * LANE, (p + 1) * LANE)
            q = q_ref[b * BLOCK:(b + 1) * BLOCK, sl]
            if b == 0:
                k2 = jnp.concatenate([ke_ref[:, sl], k_ref[0:BLOCK, sl]], axis=0)
                v2 = jnp.concatenate([ve_ref[:, sl], v_ref[0:BLOCK, sl]], axis=0)
            else:
                k2 = k_ref[(b - 1) * BLOCK:(b + 1) * BLOCK, sl]
                v2 = v_ref[(b - 1) * BLOCK:(b + 1) * BLOCK, sl]
            outs, lses = [], []
            for hh in range(2):
                h = 2 * p + hh
                sel = lo if hh == 0 else jnp.logical_not(lo)
                qm = jnp.where(sel, q, jnp.zeros_like(q))
                s = lax.dot_general(qm, k2, NT_DIMS, preferred_element_type=F32) + bias_ref[h]
                if b == 0:
                    s = jnp.where(first_ok, s, NEG)
                m = jnp.max(s, axis=-1, keepdims=True)
                e = jnp.exp(s - m)
                den = jnp.sum(e, axis=-1, keepdims=True)
                num = jnp.dot(e.astype(BF16), v2, preferred_element_type=F32)
                outs.append(num / den)
                lses.append(jnp.broadcast_to(m + jnp.log(den), (BLOCK, LANE)))
            o_ref[b * BLOCK:(b + 1) * BLOCK, sl] = jnp.where(lo, outs[0], outs[1])
            lse_ref[b * BLOCK:(b + 1) * BLOCK, sl] = jnp.where(lo, lses[0], lses[1])


def _dil_body(*refs, nblk):
    j = pl.program_id(0)
    ng = len(DIL_GROUPS)
    for g, (_, d) in enumerate(DIL_GROUPS):
        n = j % (DIL_STEPS // d)
        _dil_group(n, *refs[6 * g:6 * g + 6], *refs[6 * ng + 2 * g:6 * ng + 2 * g + 2], nblk)


def _dilated_attention(a_groups, biases):
    s = a_groups[0].shape[0] * DIL_GROUPS[0][1]
    tr = s // DIL_STEPS
    nblk = tr // BLOCK
    cb = G_W // A_OUT
    in_specs, out_specs, out_shape = [], [], []
    for (_, d) in DIL_GROUPS:
        per = DIL_STEPS // d

        def cur(off, per=per):
            return pl.BlockSpec((tr, A_OUT), lambda j: (j % per, (j // per) * cb + off))

        def edge(off, per=per):
            return pl.BlockSpec((BLOCK, A_OUT),
                                lambda j: (jnp.maximum((j % per) * nblk - 1, 0), (j // per) * cb + off))

        in_specs += [cur(0), cur(1), edge(1), cur(2), edge(2),
                     _const_spec((A_HEADS_PER_GROUP, BLOCK, 2 * BLOCK), (0, 0, 0))]
        out_specs += [pl.BlockSpec((tr, A_OUT), lambda j, per=per: (j % per, j // per))] * 2
        out_shape += [jax.ShapeDtypeStruct((s // d, d * A_OUT), F32)] * 2
    args = []
    for a_g, bias_g in zip(a_groups, biases):
        args += [a_g] * 5 + [bias_g]
    outs = pl.pallas_call(
        functools.partial(_dil_body, nblk=nblk),
        grid=(DIL_STEPS,),
        in_specs=in_specs,
        out_specs=out_specs,
        out_shape=out_shape,
        compiler_params=_params(("parallel",)),
        name="dilated_attn",
    )(*args)
    return outs[0::2], outs[1::2]


SB_TQ = 1024
SB_TK = 256
SB_ND = SB_TQ // SB_TK
LOG2E = 1.4426950408889634
SB_SOFTPLUS_LINEAR = 64.0
SB_UNDERFLOW_LOG2 = -160.0


def _sb_body(q_ref, k_ref, v_ref, tri_ref, o_ref, acc_ref, c_ref, z_ref, w_ref, live_ref):
    i = pl.program_id(1)
    qi = i * SB_TQ
    acc_ref[...] = jnp.zeros_like(acc_ref)
    c_ref[...] = jnp.zeros_like(c_ref)

    def key_start(t):
        return pl.multiple_of(qi + (SB_ND - 1 - t) * SB_TK, SB_TK)

    def first_row(t):
        return max(0, SB_ND - 1 - t) * SB_TK if isinstance(t, int) else 0

    def p1(t, slot):
        r0 = first_row(t)
        kb = k_ref[pl.ds(key_start(t), SB_TK), :]
        z_ref[slot, r0:, :] = lax.dot_general(q_ref[r0:, :], kb, NT_DIMS, preferred_element_type=F32)

    def p23(t, slot, masked):
        r0 = first_row(t)
        z = z_ref[slot, r0:, :]
        sp = jnp.where(z > SB_SOFTPLUS_LINEAR, z, jnp.log2(1.0 + jnp.exp2(z)))
        c = c_ref[r0:, :]
        u0 = z - sp + jnp.concatenate([c, c], axis=1)
        if masked:
            shape = (SB_TQ - r0, SB_TK)
            row = lax.broadcasted_iota(jnp.int32, shape, 0) + r0
            col = lax.broadcasted_iota(jnp.int32, shape, 1)
            mask = (col + (SB_ND - 1 - t) * SB_TK) < row
            sp = jnp.where(mask, sp, 0.0)
            u0 = jnp.where(mask, u0, NEG)
        c_ref[r0:, :] = c - jnp.sum(sp, axis=-1, keepdims=True)
        live_ref[0] = (jnp.max(c_ref[r0:, :]) > SB_UNDERFLOW_LOG2).astype(jnp.int32)
        u = u0 + jnp.dot(sp.astype(BF16), tri_ref[...], preferred_element_type=F32)
        w_ref[slot, r0:, :] = jnp.exp2(u.astype(BF16))

    def p4(t, slot):
        r0 = first_row(t)
        vb = v_ref[pl.ds(key_start(t), SB_TK), :]
        acc_ref[r0:, :] += jnp.dot(w_ref[slot, r0:, :], vb, preferred_element_type=F32)

    n = (i + 1) * SB_ND

    def trip(it, slot, masked=False, stages=(True, True, True)):
        if stages[2]:
            p4(it - 2, slot)
        if stages[1]:
            p23(it - 1, 1 - slot, masked)
        if stages[0]:
            p1(it, slot)

    @pl.when(i == 0)
    def _():
        for t in range(SB_ND):
            p1(t, t % 2)
            p23(t, t % 2, True)
            p4(t, t % 2)

    @pl.when(i > 0)
    def _():
        for it in range(SB_ND + 1):
            trip(it, it % 2, True, (True, it >= 1, it >= 2))

        def cond(carry):
            it, live = carry
            return jnp.logical_and(it < n, live > 0)

        def body(carry):
            it, _ = carry
            for slot in range(2):
                @pl.when(it % 2 == slot)
                def _():
                    trip(it, slot)
            return it + 1, live_ref[0]

        it_end, live = lax.while_loop(cond, body, (jnp.int32(SB_ND + 1), live_ref[0]))
        for slot in range(2):
            @pl.when(it_end % 2 == slot)
            def _():
                p4(it_end - 2, slot)

            @pl.when(jnp.logical_and(it_end % 2 == slot, live > 0))
            def _():
                p23(it_end - 1, 1 - slot, False)
                p4(it_end - 1, 1 - slot)

    o_ref[...] = acc_ref[...].astype(o_ref.dtype)


def _stick_breaking(pm, tri):
    s = pm.shape[0]
    qb, kb, vb = OFF_SQ // SB_HEAD_DIM, OFF_SK // SB_HEAD_DIM, OFF_SV // SB_HEAD_DIM
    return pl.pallas_call(
        _sb_body,
        grid=(SB_HEADS, s // SB_TQ),
        in_specs=[
            pl.BlockSpec((SB_TQ, SB_HEAD_DIM), lambda h, i: (i, qb + h)),
            pl.BlockSpec((s, SB_HEAD_DIM), lambda h, i: (0, kb + h)),
            pl.BlockSpec((s, SB_HEAD_DIM), lambda h, i: (0, vb + h)),
            _const_spec((SB_TK, SB_TK), (0, 0)),
        ],
        out_specs=pl.BlockSpec((SB_TQ, SB_HEAD_DIM), lambda h, i: (i, h)),
        out_shape=jax.ShapeDtypeStruct((s, SB_WIDTH), BF16),
        scratch_shapes=[pltpu.VMEM((SB_TQ, SB_HEAD_DIM), F32), pltpu.VMEM((SB_TQ, LANE), F32),
                        pltpu.VMEM((2, SB_TQ, SB_TK), F32), pltpu.VMEM((2, SB_TQ, SB_TK), BF16),
                        pltpu.SMEM((1,), jnp.int32)],
        compiler_params=_params(("parallel", "parallel")),
        name="stick_breaking",
    )(pm, pm, pm, tri)


RET_NC = 8


def _ret_body(q_ref, k_ref, v_ref, g_ref, cos_ref, sin_ref, perm_ref, dec_ref, zeta_ref, xi_ref,
              cd_ref, nw_ref, o_ref, r_ref, *, nc):
    n = pl.program_id(0)

    @pl.when(n == 0)
    def _():
        r_ref[...] = jnp.zeros_like(r_ref)

    cos = jnp.concatenate([cos_ref[...]] * 2, axis=1)
    sin = jnp.concatenate([sin_ref[...]] * 2, axis=1)
    perm = perm_ref[...]

    def rot(x):
        swapped = jnp.dot(x, perm, preferred_element_type=F32)
        return x.astype(F32) * cos + swapped * sin

    rq = rot(q_ref[...])
    rk = rot(k_ref[...])
    qb = rq.astype(BF16)
    qx = (rq * xi_ref[...]).astype(BF16)
    kb = rk.astype(BF16)
    kz = (rk * zeta_ref[...]).astype(BF16)
    lane = lax.broadcasted_iota(jnp.int32, (RET_CHUNK, LANE), 1)
    lo = lane < RET_DK
    zero = jnp.zeros((RET_CHUNK, LANE), BF16)
    for h in range(RET_HEADS):
        sl = slice((h // 2) * LANE, (h // 2 + 1) * LANE)
        sel = lo if h % 2 == 0 else jnp.logical_not(lo)
        hs = slice(h * RET_DV, (h + 1) * RET_DV)
        r = r_ref[h]
        for c in range(nc):
            rs = slice(c * RET_CHUNK, (c + 1) * RET_CHUNK)
            v = v_ref[rs, hs]
            qm = jnp.where(sel, qb[rs, sl], zero)
            qxm = jnp.where(sel, qx[rs, sl], zero)
            intra = lax.dot_general(qm, kb[rs, sl], NT_DIMS, preferred_element_type=F32) * dec_ref[h]
            o = jnp.dot(intra.astype(BF16), v, preferred_element_type=F32)
            o = o + jnp.dot(qxm, r.astype(BF16), preferred_element_type=F32)
            kv = lax.dot_general(kz[rs, sl], v, TN_DIMS, preferred_element_type=F32)
            r = r * cd_ref[h] + kv
            mu = jnp.mean(o, axis=-1, keepdims=True)
            d = o - mu
            var = jnp.mean(d * d, axis=-1, keepdims=True)
            on = (d * lax.rsqrt(var + EPS)) * nw_ref[:, hs]
            gate = g_ref[rs, hs].astype(F32)
            o_ref[rs, hs] = (gate * jax.nn.sigmoid(gate) * on).astype(o_ref.dtype)
        r_ref[h] = r


def _retention(pm, tabs, ret_nw3, l):
    s = pm.shape[0]
    cos_t, sin_t, perm, dec, zeta, xi, cd = tabs
    nc = min(RET_NC, s // RET_CHUNK)
    rows = nc * RET_CHUNK
    zeta_t, xi_t = jnp.tile(zeta, (nc, 1)), jnp.tile(xi, (nc, 1))
    return pl.pallas_call(
        functools.partial(_ret_body, nc=nc),
        grid=(s // rows,),
        in_specs=[
            pl.BlockSpec((rows, RET_QK_WIDTH), lambda n: (n, OFF_RQ // RET_QK_WIDTH)),
            pl.BlockSpec((rows, RET_QK_WIDTH), lambda n: (n, OFF_RK // RET_QK_WIDTH)),
            pl.BlockSpec((rows, RET_V_WIDTH), lambda n: (n, OFF_RV // RET_V_WIDTH)),
            pl.BlockSpec((rows, RET_V_WIDTH), lambda n: (n, OFF_RG // RET_V_WIDTH)),
            pl.BlockSpec((rows, LANE), lambda n: (n, 0)),
            pl.BlockSpec((rows, LANE), lambda n: (n, 0)),
            _const_spec((RET_QK_WIDTH, RET_QK_WIDTH), (0, 0)),
            _const_spec((RET_HEADS, RET_CHUNK, RET_CHUNK), (0, 0, 0)),
            _const_spec((rows, RET_QK_WIDTH), (0, 0)),
            _const_spec((rows, RET_QK_WIDTH), (0, 0)),
            _const_spec((RET_HEADS, LANE, RET_DV), (0, 0, 0)),
            _const_spec((None, 1, RET_V_WIDTH), (l, 0, 0)),
        ],
        out_specs=pl.BlockSpec((rows, RET_V_WIDTH), lambda n: (n, 0)),
        out_shape=jax.ShapeDtypeStruct((s, RET_V_WIDTH), BF16),
        scratch_shapes=[pltpu.VMEM((RET_HEADS, LANE, RET_DV), F32)],
        compiler_params=_params(("arbitrary",)),
        name="retention",
    )(pm, pm, pm, pm, cos_t, sin_t, perm, dec, zeta_t, xi_t, cd, ret_nw3)


MF_TM = 512
FF_CH = 256
CARRY = 8
MF_VMEM_LIMIT = 60 * 1024 * 1024


def _gelu_tanh(x):
    return 0.5 * x * (1.0 + jnp.tanh(np.sqrt(2.0 / np.pi).astype(np.float32) * (x + 0.044715 * (x * x * x))))


def _rms(x, w):
    return x * lax.rsqrt(jnp.mean(x * x, axis=-1, keepdims=True) + EPS) * w


def _mix_ffn_body(x_ref, o0, o1, o2, l0, l1, l2, yb_ref, yc_ref, nm_ref, wgi_ref, bg_ref, wa_ref, wb_ref, wc_ref,
                  wo_ref, nw_ref, wu_ref, wg_ref, cw_ref, cb_ref, wd_ref, fw_ref, *rest, final):
    if final:
        out_ref, so1, so2, sl1, sl2, u_ref, carry_ref, a_ref = rest
    else:
        w_next_ref, out_ref, w_main_bf_ref, w_gate_bf_ref, so1, so2, sl1, sl2, u_ref, carry_ref, a_ref = rest
        w_main_bf_ref[...] = w_next_ref[:, :MAIN_W].astype(BF16)
        w_gate_bf_ref[...] = w_next_ref[:, MAIN_W:].astype(BF16)
    i = pl.program_id(0)

    @pl.when(i == 0)
    def _():
        carry_ref[...] = jnp.zeros_like(carry_ref)

    for src, dst, d in ((o1, so1, DIL_GROUPS[1][1]), (l1, sl1, DIL_GROUPS[1][1]),
                        (o2, so2, DIL_GROUPS[2][1]), (l2, sl2, DIL_GROUPS[2][1])):
        for r in range(d):
            for j in range(A_OUT // LANE):
                col = r * A_OUT + j * LANE
                dst[j, pl.ds(r, MF_TM // d, stride=d), :] = src[:, col:col + LANE]

    def tok(ref):
        return jnp.concatenate([ref[j] for j in range(A_OUT // LANE)], axis=1)

    a0, a1, a2 = l0[...], tok(sl1), tok(sl2)
    m = jnp.maximum(jnp.maximum(a0, a1), a2)
    e0, e1, e2 = jnp.exp(a0 - m), jnp.exp(a1 - m), jnp.exp(a2 - m)
    ya = (e0 * o0[...] + e1 * tok(so1) + e2 * tok(so2)) / (e0 + e1 + e2)
    pa = jnp.dot(ya.astype(BF16), wa_ref[...], preferred_element_type=F32)
    pb = jnp.dot(yb_ref[...], wb_ref[...], preferred_element_type=F32)
    pc = jnp.dot(yc_ref[...], wc_ref[...], preferred_element_type=F32)
    x_in = x_ref[...]
    h_mix = _rms(x_in, nm_ref[...]).astype(BF16)
    merged = None
    for b, pr in enumerate((pa, pb, pc)):
        gate_in = jnp.dot(h_mix, wgi_ref[:, b * D_MODEL:(b + 1) * D_MODEL], preferred_element_type=F32)
        gate = jax.nn.sigmoid(gate_in + bg_ref[b:b + 1, :])
        merged = gate * pr if merged is None else merged + gate * pr
    x = x_in + jnp.dot(merged.astype(BF16), wo_ref[...], preferred_element_type=F32)

    h = _rms(x, nw_ref[...]).astype(BF16)
    for c in range(D_FF // FF_CH):
        sl = slice(c * FF_CH, (c + 1) * FF_CH)
        ub = u_ref.at[c % 2]
        ub[0:CARRY, :] = carry_ref[:, sl]
        ub[CARRY:CARRY + MF_TM, :] = jnp.dot(h, wu_ref[:, sl], preferred_element_type=F32)
        u0 = ub[CARRY:CARRY + MF_TM, :]
        u1 = ub[CARRY - 1:CARRY - 1 + MF_TM, :]
        u2 = ub[CARRY - 2:CARRY - 2 + MF_TM, :]
        carry_ref[:, sl] = ub[MF_TM:MF_TM + CARRY, :]
        uc = cw_ref[0:1, sl] * u2 + cw_ref[1:2, sl] * u1 + cw_ref[2:3, sl] * u0 + cb_ref[:, sl]
        g = jnp.dot(h, wg_ref[:, sl], preferred_element_type=F32)
        a_ref[:, sl] = (_gelu_tanh(uc) * g).astype(BF16)
    y = x + jnp.dot(a_ref[...], wd_ref[...], preferred_element_type=F32)
    out_ref[...] = _rms(y, fw_ref[...]) if final else y


def _mix_ffn(x, oas, lses, yb, yc, nmix3, w_gate_in, b_gate, wa, wb, wc, wo, nw3, wu, wg, cw, cb3, wd, fw, w_in, l):
    s = x.shape[0]
    tm = MF_TM
    final = l == DEPTH - 1
    row = lambda w: pl.BlockSpec((tm, w), lambda i: (i, 0))
    grp = [pl.BlockSpec((tm // d, d * A_OUT), lambda i: (i, 0)) for _, d in DIL_GROUPS]
    if not final:
        src, dst, _ = _cast_specs(w_in, l + 1, s // tm)
        br = dst.block_shape[0]
        cast_out = [pl.BlockSpec((br, w), dst.index_map) for w in (MAIN_W, GATE_W)]
        cast_shape = [jax.ShapeDtypeStruct((D_MODEL, w), BF16) for w in (MAIN_W, GATE_W)]
    return pl.pallas_call(
        functools.partial(_mix_ffn_body, final=final),
        grid=(s // tm,),
        in_specs=[row(D_MODEL)] + grp + grp + [row(SB_WIDTH), row(RET_V_WIDTH),
                  _const_spec((None, 1, D_MODEL), (l, 0, 0)),
                  _const_spec((D_MODEL, GATE_W), (0, 0)),
                  _const_spec((None, N_BRANCH, D_MODEL), (l, 0, 0)),
                  _const_spec((A_OUT, D_MODEL), (0, 0)),
                  _const_spec((SB_WIDTH, D_MODEL), (0, 0)),
                  _const_spec((RET_V_WIDTH, D_MODEL), (0, 0)),
                  _const_spec((D_MODEL, D_MODEL), (0, 0)),
                  _const_spec((None, 1, D_MODEL), (l, 0, 0)),
                  _const_spec((D_MODEL, D_FF), (0, 0)),
                  _const_spec((D_MODEL, D_FF), (0, 0)),
                  _const_spec((None, 3, D_FF), (l, 0, 0)),
                  _const_spec((None, 1, D_FF), (l, 0, 0)),
                  _const_spec((D_FF, D_MODEL), (0, 0)),
                  _const_spec((1, D_MODEL), (0, 0))] + ([] if final else [src]),
        out_specs=[row(D_MODEL)] + ([] if final else cast_out),
        out_shape=[jax.ShapeDtypeStruct((s, D_MODEL), F32)] + ([] if final else cast_shape),
        scratch_shapes=[pltpu.VMEM((A_OUT // LANE, tm, LANE), F32)] * 4 + [
            pltpu.VMEM((2, tm + CARRY, FF_CH), F32), pltpu.VMEM((CARRY, D_FF), F32),
            pltpu.VMEM((tm, D_FF), BF16)],
        compiler_params=_params(("arbitrary",), MF_VMEM_LIMIT),
        name="mix_ffn",
    )(x, *oas, *lses, yb, yc, nmix3, w_gate_in, b_gate, wa, wb, wc, wo, nw3, wu, wg, cw, cb3, wd, fw,
      *([] if final else [w_in]))


def _t5_causal_bucket(dist):
    max_exact = REL_BUCKETS // 2
    n = np.maximum(dist, 0)
    large = max_exact + (np.log(np.maximum(n, 1) / max_exact) / np.log(REL_MAX_DIST / max_exact)
                         * (REL_BUCKETS - max_exact)).astype(np.int64)
    large = np.minimum(large, REL_BUCKETS - 1)
    return np.where(n < max_exact, n, large).astype(np.int32)


def _dilated_bias(rel_bias, g, window, dilation):
    n_steps = window // dilation
    steps = np.arange(BLOCK)[:, None] + BLOCK - np.arange(2 * BLOCK)[None, :]
    in_band = (steps >= 0) & (steps <= n_steps)
    bucket = _t5_causal_bucket(steps * dilation)
    bias_g = rel_bias[:, g * A_HEADS_PER_GROUP:(g + 1) * A_HEADS_PER_GROUP].astype(F32)
    onehot = (jnp.asarray(bucket)[:, :, None] == jnp.arange(REL_BUCKETS)[None, None, :]).astype(F32)
    bias = jnp.einsum('qkb,bh->hqk', onehot, bias_g, precision=lax.Precision.HIGHEST)
    return jnp.where(jnp.asarray(in_band)[None], bias, NEG)


def _retention_tables(s):
    half = RET_DK // 2
    inv = ROPE_BASE ** (-jnp.arange(half, dtype=F32) / half)
    ang = jnp.arange(s, dtype=jnp.int32).astype(F32)[:, None] * inv[None, :]
    cos, sin = jnp.cos(ang), jnp.sin(ang)
    cos_t = jnp.concatenate([cos, cos, cos, cos], axis=-1)
    sin_t = jnp.concatenate([-sin, sin, -sin, sin], axis=-1)
    j = np.arange(RET_QK_WIDTH)
    partner = np.where((j % RET_DK) < half, j + half, j - half)
    perm = np.zeros((RET_QK_WIDTH, RET_QK_WIDTH), np.float32)
    perm[partner, j] = 1.0
    log_gamma = jnp.log1p(-jnp.exp2(-5.0 - jnp.arange(RET_HEADS, dtype=F32)))
    n = jnp.arange(RET_CHUNK, dtype=F32)
    diff = n[:, None] - n[None, :]
    dec = jnp.where(diff >= 0, jnp.exp(diff[None] * log_gamma[:, None, None]), 0.0)
    zeta = jnp.exp((RET_CHUNK - 1 - n)[:, None] * log_gamma[None, :])
    xi = jnp.exp((n + 1)[:, None] * log_gamma[None, :])
    zeta = jnp.repeat(zeta, RET_DK, axis=1)
    xi = jnp.repeat(xi, RET_DK, axis=1)
    cd = jnp.broadcast_to(jnp.exp(RET_CHUNK * log_gamma)[:, None, None], (RET_HEADS, LANE, RET_DV))
    return cos_t, sin_t, jnp.asarray(perm, BF16), dec, zeta, xi, cd


def _scale_row():
    sc = np.ones((1, MAIN_W), np.float32)
    sc[:, 0:A_WIDTH] = HEAD_DIM ** -0.5
    sc[:, A_W + OFF_SQ:A_W + OFF_SQ + SB_WIDTH] = SB_HEAD_DIM ** -0.5 * LOG2E
    sc[:, A_W + OFF_RK:A_W + OFF_RK + RET_QK_WIDTH] = RET_DK ** -0.5
    return jnp.asarray(sc)


def _tri_table():
    k = np.arange(SB_TK)[:, None]
    j = np.arange(SB_TK)[None, :]
    return jnp.asarray(-(k > j).astype(np.float32), BF16)


def kernel(x, rel_bias, norm_mix_w, w_in, b_gate, ret_norm_w, w_proj_a, w_proj_b, w_proj_c, w_out,
           norm_ffn_w, w_up, w_gate, conv_w, conv_b, w_down, final_norm_w):
    b, s, _ = x.shape
    assert b == 1
    xs = x.reshape(s, D_MODEL)
    w_main_bf, w_gate_bf = w_in[0, :, :MAIN_W].astype(BF16), w_in[0, :, MAIN_W:].astype(BF16)
    side_weights = (w_proj_a, w_proj_b, w_proj_c, w_out, w_up, w_gate, w_down)
    assert len(side_weights) == N_SIDE_CAST
    nmix3 = norm_mix_w.reshape(DEPTH, 1, D_MODEL)
    nffn3 = norm_ffn_w.reshape(DEPTH, 1, D_MODEL)
    rnw3 = ret_norm_w.reshape(DEPTH, 1, RET_V_WIDTH)
    cb3 = conv_b.reshape(DEPTH, 1, D_FF)
    scale_row = _scale_row()
    tri = _tri_table()
    biases = [_dilated_bias(rel_bias, g, w, d) for g, (w, d) in enumerate(DIL_GROUPS)]
    rtabs = _retention_tables(s)

    for l in range(DEPTH):
        a0, a1, a2, pm, wa, wb, wc, wo, wu, wg, wd = _inproj(xs, nmix3, scale_row, w_main_bf, side_weights, l)
        oas, lses = _dilated_attention((a0, a1, a2), biases)
        yb = _stick_breaking(pm, tri)
        yc = _retention(pm, rtabs, rnw3, l)
        res = _mix_ffn(xs, oas, lses, yb, yc, nmix3, w_gate_bf, b_gate, wa, wb, wc, wo, nffn3, wu, wg, conv_w, cb3,
                       wd, final_norm_w.reshape(1, D_MODEL), w_in, l)
        xs = res[0]
        if l < DEPTH - 1:
            w_main_bf, w_gate_bf = res[1], res[2]
    return xs.reshape(b, s, D_MODEL)
```
